```python
import jax, jax.numpy as jnp
from jax import lax
import numpy as np

D_MODEL = 1024
BATCH = 8
SEQ = 4096
DEPTH = 1

N_HEADS = 8
HEAD_DIM = 128
ATTN_WIDTH = N_HEADS * HEAD_DIM
Q_BLOCK = 128
LRU_WIDTH = 1536
LRU_BLOCKS = 12
LRU_BLOCK = LRU_WIDTH // LRU_BLOCKS
CONV_WIDTH = 4
LRU_C = 8.0
FFN_HIDDEN = -(-8 * D_MODEL // (3 * 256)) * 256
N_MOD = 6
EPS = 1e-6
IN_WIDTHS = (ATTN_WIDTH, ATTN_WIDTH, ATTN_WIDTH, LRU_WIDTH, LRU_WIDTH, D_MODEL, D_MODEL)
IN_TOTAL = sum(IN_WIDTHS)

kernel_name = "hybrid_stickbreak_rglru_block"


def rms_norm(x, g):
    xf = x.astype(jnp.float32)
    y = xf * lax.rsqrt(jnp.mean(xf * xf, axis=-1, keepdims=True) + EPS)
    return (y * g.astype(jnp.float32)).astype(x.dtype)


def stick_breaking_attention(q, k, v):
    S, Dh = q.shape[2], q.shape[3]
    scale = Dh ** -0.5
    outs = []
    for blk in range(S // Q_BLOCK):
        q0, q1 = blk * Q_BLOCK, (blk + 1) * Q_BLOCK
        qb = q[:, :, q0:q1]
        kb = k[:, :, :q1]
        vb = v[:, :, :q1]
        z = jnp.einsum('bhtd,bhsd->bhts', qb, kb).astype(jnp.float32) * scale
        t_idx = jnp.arange(q0, q1)[:, None]
        s_idx = jnp.arange(q1)[None, :]
        causal = s_idx < t_idx
        log_beta = jax.nn.log_sigmoid(z)
        log_one_minus = jnp.where(causal, jax.nn.log_sigmoid(-z), 0.0)
        rc = lax.cumsum(log_one_minus, axis=3, reverse=True)
        suffix = jnp.pad(rc[..., 1:], ((0, 0), (0, 0), (0, 0), (0, 1)))
        w = jnp.where(causal, jnp.exp(log_beta + suffix), 0.0)
        outs.append(jnp.einsum('bhts,bhsd->bhtd', w, vb.astype(jnp.float32)))
    return jnp.concatenate(outs, axis=2).astype(q.dtype)


def causal_depthwise_conv(x, w, b):
    S = x.shape[1]
    xp = jnp.pad(x, ((0, 0), (CONV_WIDTH - 1, 0), (0, 0)))
    y = b
    for kk in range(CONV_WIDTH):
        y = y + xp[:, kk:kk + S] * w[kk]
    return y


def block_diag_linear(x, w, b):
    Bsz, S, W = x.shape
    xb = x.reshape(Bsz, S, LRU_BLOCKS, LRU_BLOCK)
    return jnp.einsum('bsni,nij->bsnj', xb, w).reshape(Bsz, S, W) + b


def rg_lru(x, w_rg, b_rg, w_ig, b_ig, lam):
    r = jax.nn.sigmoid(block_diag_linear(x, w_rg, b_rg).astype(jnp.float32))
    i = jax.nn.sigmoid(block_diag_linear(x, w_ig, b_ig).astype(jnp.float32))
    log_a = -LRU_C * r * jax.nn.softplus(-lam.astype(jnp.float32))
    a = jnp.exp(log_a)
    mult = jnp.sqrt(-jnp.expm1(2.0 * log_a))
    u = mult * (i * x.astype(jnp.float32))

    def combine(left, right):
        a1, b1 = left
        a2, b2 = right
        return a1 * a2, a2 * b1 + b2

    _, h = lax.associative_scan(combine, (a, u), axis=1)
    return h.astype(x.dtype)


def setup_inputs(seed: int = 0) -> dict:
    key = jax.random.key(seed)
    ks = jax.random.split(key, 24)
    nrm = lambda k, shape, s: jax.random.normal(k, shape, jnp.float32) * s
    L, D = DEPTH, D_MODEL
    u = jax.random.uniform(ks[13], (L, LRU_WIDTH), jnp.float32, 0.9, 0.999)
    a0 = u ** (1.0 / LRU_C)
    lru_lambda = jnp.log(a0) - jnp.log1p(-a0)
    return {
        "x": nrm(ks[0], (BATCH, SEQ, D), 1.0),
        "c": nrm(ks[1], (BATCH, D), 1.0),
        "w_ada": nrm(ks[2], (L, D, N_MOD * D), 0.5 * D ** -0.5),
        "b_ada": nrm(ks[3], (L, N_MOD * D), 0.01),
        "norm1_g": 1.0 + nrm(ks[4], (L, D), 0.02),
        "w_in": nrm(ks[5], (L, D, IN_TOTAL), D ** -0.5),
        "q_norm_g": 1.0 + nrm(ks[6], (L, HEAD_DIM), 0.02),
        "k_norm_g": 1.0 + nrm(ks[7], (L, HEAD_DIM), 0.02),
        "conv_w": nrm(ks[8], (L, CONV_WIDTH, LRU_WIDTH), CONV_WIDTH ** -0.5),
        "conv_b": nrm(ks[9], (L, LRU_WIDTH), 0.01),
        "w_rg": nrm(ks[10], (L, LRU_BLOCKS, LRU_BLOCK, LRU_BLOCK), LRU_BLOCK ** -0.5),
        "b_rg": nrm(ks[11], (L, LRU_WIDTH), 0.01),
        "w_ig": nrm(ks[12], (L, LRU_BLOCKS, LRU_BLOCK, LRU_BLOCK), LRU_BLOCK ** -0.5),
        "b_ig": nrm(ks[14], (L, LRU_WIDTH), 0.01),
        "lru_lambda": lru_lambda,
        "w_proj_attn": nrm(ks[15], (L, ATTN_WIDTH, D), ATTN_WIDTH ** -0.5),
        "w_proj_lru": nrm(ks[16], (L, LRU_WIDTH, D), LRU_WIDTH ** -0.5),
        "w_out": nrm(ks[17], (L, D, D), D ** -0.5),
        "norm2_g": 1.0 + nrm(ks[18], (L, D), 0.02),
        "w_ffn_in": nrm(ks[19], (L, D, 2 * FFN_HIDDEN), D ** -0.5),
        "w_ffn_out": nrm(ks[20], (L, FFN_HIDDEN, D), FFN_HIDDEN ** -0.5),
    }


def reference(x, c, w_ada, b_ada, norm1_g, w_in, q_norm_g, k_norm_g, conv_w, conv_b,
              w_rg, b_rg, w_ig, b_ig, lru_lambda, w_proj_attn, w_proj_lru, w_out,
              norm2_g, w_ffn_in, w_ffn_out):
    Bsz, S, D = x.shape
    split_at = [int(v) for v in np.cumsum(IN_WIDTHS)[:-1]]
    c_act = jax.nn.silu(c)
    for l in range(DEPTH):
        mod = jnp.einsum('bd,de->be', c_act, w_ada[l]) + b_ada[l]
        shift1, scale1, gate1, shift2, scale2, gate2 = [
            m[:, None, :] for m in jnp.split(mod, N_MOD, axis=-1)]

        h = rms_norm(x, norm1_g[l]) * (1.0 + scale1) + shift1
        proj = jnp.einsum('bsd,de->bse', h, w_in[l])
        q, k, v, xr, gr, ga, gb = jnp.split(proj, split_at, axis=-1)

        q = rms_norm(q.reshape(Bsz, S, N_HEADS, HEAD_DIM), q_norm_g[l])
        k = rms_norm(k.reshape(Bsz, S, N_HEADS, HEAD_DIM), k_norm_g[l])
        v = v.reshape(Bsz, S, N_HEADS, HEAD_DIM)
        o = stick_breaking_attention(q.transpose(0, 2, 1, 3), k.transpose(0, 2, 1, 3),
                                     v.transpose(0, 2, 1, 3))
        o = o.transpose(0, 2, 1, 3).reshape(Bsz, S, ATTN_WIDTH)
        p_attn = jnp.einsum('bsa,ad->bsd', o, w_proj_attn[l])

        xc = causal_depthwise_conv(xr, conv_w[l], conv_b[l])
        y_lru = rg_lru(xc, w_rg[l], b_rg[l], w_ig[l], b_ig[l], lru_lambda[l])
        y_lru = jax.nn.gelu(gr) * y_lru
        p_lru = jnp.einsum('bsw,wd->bsd', y_lru, w_proj_lru[l])

        merged = jax.nn.sigmoid(ga) * p_attn + jax.nn.sigmoid(gb) * p_lru
        mix_out = jnp.einsum('bsd,de->bse', merged, w_out[l])
        x = x + gate1 * mix_out

        h2 = rms_norm(x, norm2_g[l]) * (1.0 + scale2) + shift2
        gu = jnp.einsum('bsd,df->bsf', h2, w_ffn_in[l])
        g_ffn, u_ffn = jnp.split(gu, 2, axis=-1)
        ffn_out = jnp.einsum('bsf,fd->bsd', jax.nn.silu(g_ffn) * u_ffn, w_ffn_out[l])
        x = x + gate2 * ffn_out
    return x
```

```python
import functools

import jax
import jax.numpy as jnp
from jax import lax
from jax.experimental import pallas as pl
from jax.experimental.pallas import tpu as pltpu

F32 = jnp.float32
BF16 = jnp.bfloat16

N_HEADS = 8
HEAD_DIM = 128
LRU_BLOCKS = 12
LRU_BLOCK = 128
CONV_WIDTH = 4
LRU_C = 8.0
N_MOD = 6
EPS = 1e-6

V7X_VMEM_LIMIT_BYTES = 56 * 1024 * 1024
LANES = 128
SUBLANES = 8

ADA_TN = 1024
INPROJ_TM = 512
ATTN_TQ = 256
ATTN_TK = 256
LRU_TS = 256
MERGE_TM = 512
FFN_TM = 512
FFN_CHUNK = 1408


def _params(*sem):
    return pltpu.CompilerParams(dimension_semantics=sem, vmem_limit_bytes=V7X_VMEM_LIMIT_BYTES)


def _resident(shape):
    nd = len(shape)
    return pl.BlockSpec(shape, lambda *_: (0,) * nd, pipeline_mode=pl.Buffered(1))


def _bdot(a, b):
    return jnp.dot(a, b, preferred_element_type=F32)


def _rms(x, g):
    ms = jnp.mean(x * x, axis=-1, keepdims=True)
    return x * lax.rsqrt(ms + EPS) * g


def _ada_kernel(c_ref, w_ref, b_ref, o_ref):
    c = c_ref[...]
    ca = c * jax.nn.sigmoid(c)
    o_ref[...] = jnp.dot(ca, w_ref[...], preferred_element_type=F32,
                         precision=lax.Precision.HIGHEST) + b_ref[...]


def _ada(c, w, b):
    bsz, d = c.shape
    n = w.shape[1]
    return pl.pallas_call(
        _ada_kernel,
        out_shape=jax.ShapeDtypeStruct((bsz, n), F32),
        grid=(n // ADA_TN,),
        in_specs=[pl.BlockSpec((bsz, d), lambda j: (0, 0)),
                  pl.BlockSpec((d, ADA_TN), lambda j: (0, j)),
                  pl.BlockSpec((1, ADA_TN), lambda j: (0, j))],
        out_specs=pl.BlockSpec((bsz, ADA_TN), lambda j: (0, j)),
        compiler_params=_params("arbitrary"),
        name="ada",
    )(c, w, b.reshape(1, n))


def _inproj_kernel(x_ref, mod_ref, g1_ref, w_ref, qg_ref, kg_ref,
                   q_ref, k_ref, v_ref, xr_ref, gr_ref, ga_ref, gb_ref, *, widths):
    x = x_ref[0]
    shift = mod_ref[0, 0:1, :]
    scale = mod_ref[0, 1:2, :]
    h = (_rms(x, g1_ref[...]) * (1.0 + scale) + shift).astype(BF16)

    offs = [0]
    for wd in widths:
        offs.append(offs[-1] + wd)

    def proj(idx):
        return _bdot(h, w_ref[:, offs[idx]:offs[idx + 1]])

    q_scale = HEAD_DIM ** -0.5
    for idx, g_ref, o_ref, mul in ((0, qg_ref, q_ref, q_scale), (1, kg_ref, k_ref, 1.0)):
        p = proj(idx)
        for hh in range(N_HEADS):
            sl = slice(hh * HEAD_DIM, (hh + 1) * HEAD_DIM)
            ph = _rms(p[:, sl], g_ref[...])
            if mul != 1.0:
                ph = ph * mul
            o_ref[0, :, sl] = ph.astype(BF16)
    v_ref[0] = proj(2).astype(BF16)
    xr_ref[0] = proj(3)
    gr_ref[0] = jax.nn.gelu(proj(4)).astype(BF16)
    ga_ref[0] = jax.nn.sigmoid(proj(5)).astype(BF16)
    gb_ref[0] = jax.nn.sigmoid(proj(6)).astype(BF16)


def _inproj(x, mod, g1, w_in, qg, kg, widths):
    bsz, s, d = x.shape
    tm = INPROJ_TM
    tok = lambda wd: pl.BlockSpec((1, tm, wd), lambda b, i: (b, i, 0))
    out_dtypes = (BF16, BF16, BF16, F32, BF16, BF16, BF16)
    return pl.pallas_call(
        functools.partial(_inproj_kernel, widths=widths),
        out_shape=[jax.ShapeDtypeStruct((bsz, s, wd), dt) for wd, dt in zip(widths, out_dtypes)],
        grid=(bsz, s // tm),
        in_specs=[tok(d),
                  pl.BlockSpec((1, N_MOD, d), lambda b, i: (b, 0, 0)),
                  _resident((1, d)),
                  _resident(w_in.shape),
                  _resident((1, HEAD_DIM)),
                  _resident((1, HEAD_DIM))],
        out_specs=[tok(wd) for wd in widths],
        compiler_params=_params("parallel", "arbitrary"),
        name="inproj",
    )(x, mod, g1, w_in, qg, kg)


def _attn_kernel(q_ref, k_ref, v_ref, tri_ref, o_ref, acc_ref, carry_ref):
    tq, tk = ATTN_TQ, ATTN_TK
    i = pl.program_id(2)
    q = q_ref[0]
    acc_ref[...] = jnp.zeros_like(acc_ref)
    carry_ref[...] = jnp.zeros_like(carry_ref)

    def block(j, diagonal):
        k0 = pl.multiple_of(j * tk, tk)
        kb = k_ref[0, pl.ds(k0, tk), :]
        vb = v_ref[0, pl.ds(k0, tk), :]
        z = lax.dot_general(q, kb, (((1,), (1,)), ((), ())), preferred_element_type=F32)
        sp = jnp.log(1.0 + jnp.exp(-jnp.abs(z)))
        log_beta = jnp.minimum(z, 0.0) - sp
        log_om = log_beta - z
        if diagonal:
            row = lax.broadcasted_iota(jnp.int32, (tq, tk), 0)
            col = lax.broadcasted_iota(jnp.int32, (tq, tk), 1)
            causal = col < row
            log_om = jnp.where(causal, log_om, 0.0)
        hi = log_om.astype(BF16)
        lo = (log_om - hi.astype(F32)).astype(BF16)
        suffix = _bdot(jnp.concatenate([hi, lo], axis=1), tri_ref[...])
        carry = carry_ref[...]
        e = log_beta + suffix + jnp.concatenate([carry] * (tk // LANES), axis=1)
        w = jnp.exp(e)
        if diagonal:
            w = jnp.where(causal, w, 0.0)
        acc_ref[...] += _bdot(w.astype(BF16), vb)
        tot = jnp.sum(log_om, axis=1, keepdims=True)
        carry_ref[...] = carry + jnp.broadcast_to(tot, carry.shape)

    block(i, True)

    def body(it, _):
        block(i - 1 - it, False)
        return 0

    lax.fori_loop(0, i, body, 0)
    o_ref[0] = acc_ref[...].astype(o_ref.dtype)


def _attn(q, k, v):
    bsz, s, width = q.shape
    tq, tk = ATTN_TQ, ATTN_TK
    assert tq == tk and s % tq == 0
    r = lax.broadcasted_iota(jnp.int32, (tk, tk), 0)
    cidx = lax.broadcasted_iota(jnp.int32, (tk, tk), 1)
    tri = (r > cidx).astype(BF16)
    tri2 = jnp.concatenate([tri, tri], axis=0)
    return pl.pallas_call(
        _attn_kernel,
        out_shape=jax.ShapeDtypeStruct((bsz, s, width), BF16),
        grid=(bsz, N_HEADS, s // tq),
        in_specs=[pl.BlockSpec((1, tq, HEAD_DIM), lambda b, h, i: (b, i, h)),
                  pl.BlockSpec((1, s, HEAD_DIM), lambda b, h, i: (b, 0, h)),
                  pl.BlockSpec((1, s, HEAD_DIM), lambda b, h, i: (b, 0, h)),
                  _resident((2 * tk, tk))],
        out_specs=pl.BlockSpec((1, tq, HEAD_DIM), lambda b, h, i: (b, i, h)),
        scratch_shapes=[pltpu.VMEM((tq, HEAD_DIM), F32),
                        pltpu.VMEM((tq, LANES), F32)],
        compiler_params=_params("parallel", "parallel", "arbitrary"),
        name="attn",
    )(q, k, v, tri2)


def _lru_kernel(xr_ref, gate_ref, cw_ref, cb_ref, wg_ref, bg_ref, lam_ref, y_ref,
                xbuf, a_s, b_s, h_s, hlast):
    ts = LRU_TS
    hdr = SUBLANES

    @pl.when(pl.program_id(1) == 0)
    def _():
        xbuf[0:hdr, :] = jnp.zeros((hdr, xbuf.shape[1]), F32)
        hlast[...] = jnp.zeros_like(hlast)

    xbuf[hdr:hdr + ts, :] = xr_ref[0]
    row = lax.broadcasted_iota(jnp.int32, (ts, LRU_BLOCK), 0) & (SUBLANES - 1)

    for n in range(LRU_BLOCKS):
        sl = slice(n * LRU_BLOCK, (n + 1) * LRU_BLOCK)
        xc = cb_ref[:, sl]
        for kk in range(CONV_WIDTH):
            start = hdr - (CONV_WIDTH - 1 - kk)
            xc = xc + xbuf[start:start + ts, sl] * cw_ref[kk:kk + 1, sl]
        g = _bdot(xc.astype(BF16), wg_ref[n]) + bg_ref[n]
        r = jax.nn.sigmoid(g[:, :LRU_BLOCK])
        ig = jax.nn.sigmoid(g[:, LRU_BLOCK:])
        neg_lam = -lam_ref[:, sl]
        softplus = jnp.maximum(neg_lam, 0.0) + jnp.log(1.0 + jnp.exp(-jnp.abs(neg_lam)))
        log_a = -LRU_C * r * softplus
        a = jnp.exp(log_a)
        mult = jnp.sqrt(-jnp.tanh(log_a) * (1.0 + a * a))
        u = mult * (ig * xc)
        for dist in (1, 2, 4):
            inside = row >= dist
            a_prev = jnp.where(inside, pltpu.roll(a, dist, 0), 1.0)
            u_prev = jnp.where(inside, pltpu.roll(u, dist, 0), 0.0)
            u = a * u_prev + u
            a = a * a_prev
        a_s[:, sl] = a
        b_s[:, sl] = u

    xbuf[0:hdr, :] = xbuf[ts:ts + hdr, :]

    def body(gidx, h):
        r0 = pl.multiple_of(gidx * SUBLANES, SUBLANES)
        hh = a_s[pl.ds(r0, SUBLANES), :] * h + b_s[pl.ds(r0, SUBLANES), :]
        h_s[pl.ds(r0, SUBLANES), :] = hh
        return hh[SUBLANES - 1:SUBLANES, :]

    hlast[...] = lax.fori_loop(0, ts // SUBLANES, body, hlast[...], unroll=4)
    y_ref[0] = (gate_ref[0].astype(F32) * h_s[...]).astype(y_ref.dtype)


def _lru(xr, gate, conv_w, conv_b, w_gates, b_gates, lam):
    bsz, s, w = xr.shape
    ts = LRU_TS
    tok = pl.BlockSpec((1, ts, w), lambda b, i: (b, i, 0))
    return pl.pallas_call(
        _lru_kernel,
        out_shape=jax.ShapeDtypeStruct((bsz, s, w), BF16),
        grid=(bsz, s // ts),
        in_specs=[tok, tok,
                  _resident(conv_w.shape), _resident(conv_b.shape),
                  _resident(w_gates.shape), _resident(b_gates.shape), _resident(lam.shape)],
        out_specs=tok,
        scratch_shapes=[pltpu.VMEM((ts + SUBLANES, w), F32),
                        pltpu.VMEM((ts, w), F32),
                        pltpu.VMEM((ts, w), F32),
                        pltpu.VMEM((ts, w), F32),
                        pltpu.VMEM((1, w), F32)],
        compiler_params=_params("parallel", "arbitrary"),
        name="lru",
    )(xr, gate, conv_w, conv_b, w_gates, b_gates, lam)


def _merge_kernel(o_ref, y_ref, ga_ref, gb_ref, x_ref, mod_ref, wpa_ref, wpl_ref, wo_ref, x1_ref):
    p_attn = _bdot(o_ref[0], wpa_ref[...])
    p_lru = _bdot(y_ref[0], wpl_ref[...])
    merged = ga_ref[0].astype(F32) * p_attn + gb_ref[0].astype(F32) * p_lru
    mix = _bdot(merged.astype(BF16), wo_ref[...])
    gate1 = mod_ref[0, 2:3, :]
    x1_ref[0] = x_ref[0] + gate1 * mix


def _merge(o, y, ga, gb, x, mod, wpa, wpl, wo):
    bsz, s, d = x.shape
    tm = MERGE_TM
    tok = lambda wd: pl.BlockSpec((1, tm, wd), lambda b, i: (b, i, 0))
    return pl.pallas_call(
        _merge_kernel,
        out_shape=jax.ShapeDtypeStruct((bsz, s, d), F32),
        grid=(bsz, s // tm),
        in_specs=[tok(o.shape[2]), tok(y.shape[2]), tok(d), tok(d), tok(d),
                  pl.BlockSpec((1, N_MOD, d), lambda b, i: (b, 0, 0)),
                  _resident(wpa.shape), _resident(wpl.shape), _resident(wo.shape)],
        out_specs=tok(d),
        compiler_params=_params("parallel", "arbitrary"),
        name="merge",
    )(o, y, ga, gb, x, mod, wpa, wpl, wo)


def _ffn_kernel(x_ref, mod_ref, g2_ref, wg_ref, wu_ref, wd_ref, out_ref):
    x = x_ref[0]
    shift = mod_ref[0, 3:4, :]
    scale = mod_ref[0, 4:5, :]
    gate2 = mod_ref[0, 5:6, :]
    h = (_rms(x, g2_ref[...]) * (1.0 + scale) + shift).astype(BF16)
    hidden = wg_ref.shape[1]
    acc = jnp.zeros(x.shape, F32)
    for c0 in range(0, hidden, FFN_CHUNK):
        g = _bdot(h, wg_ref[:, c0:c0 + FFN_CHUNK])
        u = _bdot(h, wu_ref[:, c0:c0 + FFN_CHUNK])
        act = (g * jax.nn.sigmoid(g) * u).astype(BF16)
        acc = acc + _bdot(act, wd_ref[c0:c0 + FFN_CHUNK, :])
    out_ref[0] = x + gate2 * acc


def _ffn(x1, mod, g2, wg, wu, wd):
    bsz, s, d = x1.shape
    tm = FFN_TM
    assert wg.shape[1] % FFN_CHUNK == 0
    tok = pl.BlockSpec((1, tm, d), lambda b, i: (b, i, 0))
    return pl.pallas_call(
        _ffn_kernel,
        out_shape=jax.ShapeDtypeStruct((bsz, s, d), F32),
        grid=(bsz, s // tm),
        in_specs=[tok,
                  pl.BlockSpec((1, N_MOD, d), lambda b, i: (b, 0, 0)),
                  _resident((1, d)),
                  _resident(wg.shape), _resident(wu.shape), _resident(wd.shape)],
        out_specs=tok,
        compiler_params=_params("parallel", "arbitrary"),
        name="ffn",
    )(x1, mod, g2, wg, wu, wd)


def kernel(x, c, w_ada, b_ada, norm1_g, w_in, q_norm_g, k_norm_g, conv_w, conv_b,
           w_rg, b_rg, w_ig, b_ig, lru_lambda, w_proj_attn, w_proj_lru, w_out,
           norm2_g, w_ffn_in, w_ffn_out):
    bsz, s, d = x.shape
    depth = w_ada.shape[0]
    attn_w = N_HEADS * HEAD_DIM
    lru_w = LRU_BLOCKS * LRU_BLOCK
    widths = (attn_w, attn_w, attn_w, lru_w, lru_w, d, d)
    hidden = w_ffn_out.shape[1]
    for l in range(depth):
        mod = _ada(c, w_ada[l], b_ada[l]).reshape(bsz, N_MOD, d)
        q, k, v, xr, gate, ga, gb = _inproj(
            x, mod, norm1_g[l].reshape(1, d), w_in[l].astype(BF16),
            q_norm_g[l].reshape(1, HEAD_DIM), k_norm_g[l].reshape(1, HEAD_DIM), widths)
        o = _attn(q, k, v)
        w_gates = jnp.concatenate([w_rg[l], w_ig[l]], axis=-1).astype(BF16)
        b_gates = jnp.concatenate([b_rg[l].reshape(LRU_BLOCKS, 1, LRU_BLOCK),
                                   b_ig[l].reshape(LRU_BLOCKS, 1, LRU_BLOCK)], axis=-1)
        y = _lru(xr, gate, conv_w[l], conv_b[l].reshape(1, lru_w), w_gates, b_gates,
                 lru_lambda[l].reshape(1, lru_w))
        x1 = _merge(o, y, ga, gb, x, mod, w_proj_attn[l].astype(BF16),
                    w_proj_lru[l].astype(BF16), w_out[l].astype(BF16))
        w_ffn = w_ffn_in[l].astype(BF16)
        x = _ffn(x1, mod, norm2_g[l].reshape(1, d), w_ffn[:, :hidden], w_ffn[:, hidden:],
                 w_ffn_out[l].astype(BF16))
    return x
```

```python
import functools

import jax
import jax.numpy as jnp
from jax import lax
from jax.experimental import pallas as pl
from jax.experimental.pallas import tpu as pltpu

F32 = jnp.float32
BF16 = jnp.bfloat16

N_HEADS = 8
HEAD_DIM = 128
LRU_BLOCKS = 12
LRU_BLOCK = 128
CONV_WIDTH = 4
LRU_C = 8.0
N_MOD = 6
EPS = 1e-6
LOG2E = 1.4426950408889634

V7X_VMEM_LIMIT_BYTES = 56 * 1024 * 1024
LANES = 128
SUBLANES = 8

ADA_TN = 1024
INPROJ_TM = 512
ATTN_TQ = 256
ATTN_TK = 256
ATTN_HP = 4
ATTN_RS = 32
LRU_TS = 256
MERGE_TM = 512
FFN_TM = 512
FFN_CHUNK = 1408


def _params(*sem):
    return pltpu.CompilerParams(dimension_semantics=sem, vmem_limit_bytes=V7X_VMEM_LIMIT_BYTES)


def _resident(shape):
    nd = len(shape)
    return pl.BlockSpec(shape, lambda *_: (0,) * nd, pipeline_mode=pl.Buffered(1))


def _bdot(a, b):
    return jnp.dot(a, b, preferred_element_type=F32)


def _rms(x, g):
    ms = jnp.mean(x * x, axis=-1, keepdims=True)
    return x * lax.rsqrt(ms + EPS) * g


def _ada_kernel(c_ref, w_ref, b_ref, o_ref):
    c = c_ref[...]
    ca = c * jax.nn.sigmoid(c)
    o_ref[...] = jnp.dot(ca, w_ref[...], preferred_element_type=F32,
                         precision=lax.Precision.HIGHEST) + b_ref[...]


def _ada(c, w, b):
    bsz, d = c.shape
    n = w.shape[1]
    return pl.pallas_call(
        _ada_kernel,
        out_shape=jax.ShapeDtypeStruct((bsz, n), F32),
        grid=(n // ADA_TN,),
        in_specs=[pl.BlockSpec((bsz, d), lambda j: (0, 0)),
                  pl.BlockSpec((d, ADA_TN), lambda j: (0, j)),
                  pl.BlockSpec((1, ADA_TN), lambda j: (0, j))],
        out_specs=pl.BlockSpec((bsz, ADA_TN), lambda j: (0, j)),
        compiler_params=_params("arbitrary"),
        name="ada",
    )(c, w, b.reshape(1, n))


def _inproj_kernel(x_ref, mod_ref, g1_ref, w_ref, qg_ref, kg_ref,
                   q_ref, k_ref, v_ref, xr_ref, gr_ref, ga_ref, gb_ref, *, widths):
    x = x_ref[0]
    shift = mod_ref[0, 0:1, :]
    scale = mod_ref[0, 1:2, :]
    h = (_rms(x, g1_ref[...]) * (1.0 + scale) + shift).astype(BF16)

    offs = [0]
    for wd in widths:
        offs.append(offs[-1] + wd)

    def proj(idx):
        return _bdot(h, w_ref[:, offs[idx]:offs[idx + 1]])

    q_scale = HEAD_DIM ** -0.5 * LOG2E
    for idx, g_ref, o_ref, mul in ((0, qg_ref, q_ref, q_scale), (1, kg_ref, k_ref, 1.0)):
        p = proj(idx)
        for hh in range(N_HEADS):
            sl = slice(hh * HEAD_DIM, (hh + 1) * HEAD_DIM)
            ph = _rms(p[:, sl], g_ref[...])
            if mul != 1.0:
                ph = ph * mul
            o_ref[0, :, sl] = ph.astype(BF16)
    v_ref[0] = proj(2).astype(BF16)
    xr_ref[0] = proj(3)
    gr_ref[0] = jax.nn.gelu(proj(4)).astype(BF16)
    ga_ref[0] = jax.nn.sigmoid(proj(5)).astype(BF16)
    gb_ref[0] = jax.nn.sigmoid(proj(6)).astype(BF16)


def _inproj(x, mod, g1, w_in, qg, kg, widths):
    bsz, s, d = x.shape
    tm = INPROJ_TM
    tok = lambda wd: pl.BlockSpec((1, tm, wd), lambda b, i: (b, i, 0))
    out_dtypes = (BF16, BF16, BF16, F32, BF16, BF16, BF16)
    return pl.pallas_call(
        functools.partial(_inproj_kernel, widths=widths),
        out_shape=[jax.ShapeDtypeStruct((bsz, s, wd), dt) for wd, dt in zip(widths, out_dtypes)],
        grid=(bsz, s // tm),
        in_specs=[tok(d),
                  pl.BlockSpec((1, N_MOD, d), lambda b, i: (b, 0, 0)),
                  _resident((1, d)),
                  _resident(w_in.shape),
                  _resident((1, HEAD_DIM)),
                  _resident((1, HEAD_DIM))],
        out_specs=[tok(wd) for wd in widths],
        compiler_params=_params("parallel", "arbitrary"),
        name="inproj",
    )(x, mod, g1, w_in, qg, kg)


def _attn_kernel(q_ref, k_ref, v_ref, tri_ref, o_ref, acc_ref, carry_ref, tot_ref, z_ref):
    tq, tk = ATTN_TQ, ATTN_TK
    i = pl.program_id(2)
    heads = range(ATTN_HP)
    lanes = [slice(hh * HEAD_DIM, (hh + 1) * HEAD_DIM) for hh in heads]
    strips = [slice(r0, r0 + ATTN_RS) for r0 in range(0, tq, ATTN_RS)]
    acc_ref[...] = jnp.zeros_like(acc_ref)
    carry_ref[...] = jnp.zeros_like(carry_ref)

    def scores(j):
        k0 = pl.multiple_of(j * tk, tk)
        for hh in heads:
            z_ref[hh] = lax.dot_general(q_ref[0, :, lanes[hh]], k_ref[0, pl.ds(k0, tk), lanes[hh]],
                                        (((1,), (1,)), ((), ())), preferred_element_type=F32)

    def block(j, diagonal):
        k0 = pl.multiple_of(j * tk, tk)
        if diagonal:
            row = lax.broadcasted_iota(jnp.int32, (tq, tk), 0)
            col = lax.broadcasted_iota(jnp.int32, (tq, tk), 1)
            causal = col < row
        log_beta, suffix = [], []
        for hh in heads:
            lbs, loms = [], []
            for rs in strips:
                zs = z_ref[hh, rs, :]
                neg_abs = pltpu.bitcast(pltpu.bitcast(zs, jnp.uint32) | jnp.uint32(0x80000000), F32)
                sp = jnp.log(1.0 + jnp.exp2(neg_abs)) * LOG2E
                lb = jnp.minimum(zs, 0.0) - sp
                log_om = lb - zs
                if diagonal:
                    log_om = jnp.where(causal[rs], log_om, 0.0)
                tot = jnp.sum(log_om, axis=1, keepdims=True)
                tot_ref[hh, rs, :] = jnp.broadcast_to(tot, (ATTN_RS, LANES))
                lbs.append(lb)
                loms.append(log_om.astype(BF16))
            suffix.append(_bdot(jnp.concatenate(loms, axis=0), tri_ref[...]))
            log_beta.append(lbs)
        scores(jnp.maximum(j - 1, 0))
        for hh in heads:
            ws = []
            for si, rs in enumerate(strips):
                carry = carry_ref[hh, rs, :]
                e = log_beta[hh][si] + suffix[hh][rs] + jnp.concatenate([carry] * (tk // LANES), axis=1)
                w = jnp.exp2(e)
                if diagonal:
                    w = jnp.where(causal[rs], w, 0.0)
                ws.append(w.astype(BF16))
                carry_ref[hh, rs, :] = carry + tot_ref[hh, rs, :]
            acc_ref[hh] += _bdot(jnp.concatenate(ws, axis=0), v_ref[0, pl.ds(k0, tk), lanes[hh]])

    scores(i)
    block(i, True)

    def body(it, _):
        block(i - 1 - it, False)
        return 0

    lax.fori_loop(0, i, body, 0)
    for hh in heads:
        o_ref[0, :, lanes[hh]] = acc_ref[hh].astype(o_ref.dtype)


def _attn(q, k, v):
    bsz, s, width = q.shape
    tq, tk = ATTN_TQ, ATTN_TK
    gw = ATTN_HP * HEAD_DIM
    assert tq == tk and s % tq == 0 and width % gw == 0
    r = lax.broadcasted_iota(jnp.int32, (tk, tk), 0)
    cidx = lax.broadcasted_iota(jnp.int32, (tk, tk), 1)
    tri = (r > cidx).astype(BF16)
    return pl.pallas_call(
        _attn_kernel,
        out_shape=jax.ShapeDtypeStruct((bsz, s, width), BF16),
        grid=(bsz, width // gw, s // tq),
        in_specs=[pl.BlockSpec((1, tq, gw), lambda b, h, i: (b, i, h)),
                  pl.BlockSpec((1, s, gw), lambda b, h, i: (b, 0, h)),
                  pl.BlockSpec((1, s, gw), lambda b, h, i: (b, 0, h)),
                  _resident((tk, tk))],
        out_specs=pl.BlockSpec((1, tq, gw), lambda b, h, i: (b, i, h)),
        scratch_shapes=[pltpu.VMEM((ATTN_HP, tq, HEAD_DIM), F32),
                        pltpu.VMEM((ATTN_HP, tq, LANES), F32),
                        pltpu.VMEM((ATTN_HP, tq, LANES), F32),
                        pltpu.VMEM((ATTN_HP, tq, tk), F32)],
        compiler_params=_params("parallel", "parallel", "arbitrary"),
        name="attn",
    )(q, k, v, tri)


def _lru_kernel(xr_ref, gate_ref, cw_ref, cb_ref, wg_ref, bg_ref, lam_ref, y_ref,
                xbuf, a_s, b_s, h_s, hlast):
    ts = LRU_TS
    hdr = SUBLANES

    @pl.when(pl.program_id(1) == 0)
    def _():
        xbuf[0:hdr, :] = jnp.zeros((hdr, xbuf.shape[1]), F32)
        hlast[...] = jnp.zeros_like(hlast)

    xbuf[hdr:hdr + ts, :] = xr_ref[0]
    groups = ts // SUBLANES
    row = lax.broadcasted_iota(jnp.int32, (groups, SUBLANES, LRU_BLOCK), 1)

    for n in range(LRU_BLOCKS):
        sl = slice(n * LRU_BLOCK, (n + 1) * LRU_BLOCK)
        xc = cb_ref[:, sl]
        for kk in range(CONV_WIDTH):
            start = hdr - (CONV_WIDTH - 1 - kk)
            xc = xc + xbuf[start:start + ts, sl] * cw_ref[kk:kk + 1, sl]
        g = _bdot(xc.astype(BF16), wg_ref[n]) + bg_ref[n]
        r = jax.nn.sigmoid(g[:, :LRU_BLOCK])
        ig = jax.nn.sigmoid(g[:, LRU_BLOCK:])
        neg_lam = -lam_ref[:, sl]
        softplus = jnp.maximum(neg_lam, 0.0) + jnp.log(1.0 + jnp.exp(-jnp.abs(neg_lam)))
        log_a = -LRU_C * r * softplus
        a = jnp.exp(log_a)
        mult = jnp.sqrt(-jnp.tanh(log_a) * (1.0 + a * a))
        u = mult * (ig * xc)
        a = a.reshape(groups, SUBLANES, LRU_BLOCK)
        u = u.reshape(groups, SUBLANES, LRU_BLOCK)
        for dist in (1, 2, 4):
            inside = row >= dist
            a_prev = jnp.where(inside, pltpu.roll(a, dist, 1), 1.0)
            u_prev = jnp.where(inside, pltpu.roll(u, dist, 1), 0.0)
            u = a * u_prev + u
            a = a * a_prev
        a_s[:, sl] = a.reshape(ts, LRU_BLOCK)
        b_s[:, sl] = u.reshape(ts, LRU_BLOCK)

    xbuf[0:hdr, :] = xbuf[ts:ts + hdr, :]

    def body(gidx, h):
        r0 = pl.multiple_of(gidx * SUBLANES, SUBLANES)
        hh = a_s[pl.ds(r0, SUBLANES), :] * h + b_s[pl.ds(r0, SUBLANES), :]
        h_s[pl.ds(r0, SUBLANES), :] = hh
        return hh[SUBLANES - 1:SUBLANES, :]

    hlast[...] = lax.fori_loop(0, ts // SUBLANES, body, hlast[...], unroll=4)
    y_ref[0] = (gate_ref[0].astype(F32) * h_s[...]).astype(y_ref.dtype)


def _lru(xr, gate, conv_w, conv_b, w_gates, b_gates, lam):
    bsz, s, w = xr.shape
    ts = LRU_TS
    tok = pl.BlockSpec((1, ts, w), lambda b, i: (b, i, 0))
    return pl.pallas_call(
        _lru_kernel,
        out_shape=jax.ShapeDtypeStruct((bsz, s, w), BF16),
        grid=(bsz, s // ts),
        in_specs=[tok, tok,
                  _resident(conv_w.shape), _resident(conv_b.shape),
                  _resident(w_gates.shape), _resident(b_gates.shape), _resident(lam.shape)],
        out_specs=tok,
        scratch_shapes=[pltpu.VMEM((ts + SUBLANES, w), F32),
                        pltpu.VMEM((ts, w), F32),
                        pltpu.VMEM((ts, w), F32),
                        pltpu.VMEM((ts, w), F32),
                        pltpu.VMEM((1, w), F32)],
        compiler_params=_params("parallel", "arbitrary"),
        name="lru",
    )(xr, gate, conv_w, conv_b, w_gates, b_gates, lam)


def _merge_kernel(o_ref, y_ref, ga_ref, gb_ref, x_ref, mod_ref, wpa_ref, wpl_ref, wo_ref, x1_ref):
    p_attn = _bdot(o_ref[0], wpa_ref[...])
    p_lru = _bdot(y_ref[0], wpl_ref[...])
    merged = ga_ref[0].astype(F32) * p_attn + gb_ref[0].astype(F32) * p_lru
    mix = _bdot(merged.astype(BF16), wo_ref[...])
    gate1 = mod_ref[0, 2:3, :]
    x1_ref[0] = x_ref[0] + gate1 * mix


def _merge(o, y, ga, gb, x, mod, wpa, wpl, wo):
    bsz, s, d = x.shape
    tm = MERGE_TM
    tok = lambda wd: pl.BlockSpec((1, tm, wd), lambda b, i: (b, i, 0))
    return pl.pallas_call(
        _merge_kernel,
        out_shape=jax.ShapeDtypeStruct((bsz, s, d), F32),
        grid=(bsz, s // tm),
        in_specs=[tok(o.shape[2]), tok(y.shape[2]), tok(d), tok(d), tok(d),
                  pl.BlockSpec((1, N_MOD, d), lambda b, i: (b, 0, 0)),
                  _resident(wpa.shape), _resident(wpl.shape), _resident(wo.shape)],
        out_specs=tok(d),
        compiler_params=_params("parallel", "arbitrary"),
        name="merge",
    )(o, y, ga, gb, x, mod, wpa, wpl, wo)


def _ffn_kernel(x_ref, mod_ref, g2_ref, wg_ref, wu_ref, wd_ref, out_ref):
    x = x_ref[0]
    shift = mod_ref[0, 3:4, :]
    scale = mod_ref[0, 4:5, :]
    gate2 = mod_ref[0, 5:6, :]
    h = (_rms(x, g2_ref[...]) * (1.0 + scale) + shift).astype(BF16)
    hidden = wg_ref.shape[1]
    acc = jnp.zeros(x.shape, F32)
    for c0 in range(0, hidden, FFN_CHUNK):
        g = _bdot(h, wg_ref[:, c0:c0 + FFN_CHUNK])
        u = _bdot(h, wu_ref[:, c0:c0 + FFN_CHUNK])
        act = (g * jax.nn.sigmoid(g) * u).astype(BF16)
        acc = acc + _bdot(act, wd_ref[c0:c0 + FFN_CHUNK, :])
    out_ref[0] = x + gate2 * acc


def _ffn(x1, mod, g2, wg, wu, wd):
    bsz, s, d = x1.shape
    tm = FFN_TM
    assert wg.shape[1] % FFN_CHUNK == 0
    tok = pl.BlockSpec((1, tm, d), lambda b, i: (b, i, 0))
    return pl.pallas_call(
        _ffn_kernel,
        out_shape=jax.ShapeDtypeStruct((bsz, s, d), F32),
        grid=(bsz, s // tm),
        in_specs=[tok,
                  pl.BlockSpec((1, N_MOD, d), lambda b, i: (b, 0, 0)),
                  _resident((1, d)),
                  _resident(wg.shape), _resident(wu.shape), _resident(wd.shape)],
        out_specs=tok,
        compiler_params=_params("parallel", "arbitrary"),
        name="ffn",
    )(x1, mod, g2, wg, wu, wd)


def kernel(x, c, w_ada, b_ada, norm1_g, w_in, q_norm_g, k_norm_g, conv_w, conv_b,
           w_rg, b_rg, w_ig, b_ig, lru_lambda, w_proj_attn, w_proj_lru, w_out,
           norm2_g, w_ffn_in, w_ffn_out):
    bsz, s, d = x.shape
    depth = w_ada.shape[0]
    attn_w = N_HEADS * HEAD_DIM
    lru_w = LRU_BLOCKS * LRU_BLOCK
    widths = (attn_w, attn_w, attn_w, lru_w, lru_w, d, d)
    hidden = w_ffn_out.shape[1]
    for l in range(depth):
        mod = _ada(c, w_ada[l], b_ada[l]).reshape(bsz, N_MOD, d)
        q, k, v, xr, gate, ga, gb = _inproj(
            x, mod, norm1_g[l].reshape(1, d), w_in[l].astype(BF16),
            q_norm_g[l].reshape(1, HEAD_DIM), k_norm_g[l].reshape(1, HEAD_DIM), widths)
        o = _attn(q, k, v)
        w_gates = jnp.concatenate([w_rg[l], w_ig[l]], axis=-1).astype(BF16)
        b_gates = jnp.concatenate([b_rg[l].reshape(LRU_BLOCKS, 1, LRU_BLOCK),
                                   b_ig[l].reshape(LRU_BLOCKS, 1, LRU_BLOCK)], axis=-1)
        y = _lru(xr, gate, conv_w[l], conv_b[l].reshape(1, lru_w), w_gates, b_gates,
                 lru_lambda[l].reshape(1, lru_w))
        x1 = _merge(o, y, ga, gb, x, mod, w_proj_attn[l].astype(BF16),
                    w_proj_lru[l].astype(BF16), w_out[l].astype(BF16))
        w_ffn = w_ffn_in[l].astype(BF16)
        x = _ffn(x1, mod, norm2_g[l].reshape(1, d), w_ffn[:, :hidden], w_ffn[:, hidden:],
                 w_ffn_out[l].astype(BF16))
    return x
```

```python
import functools

import jax
import jax.numpy as jnp
from jax import lax
from jax.experimental import pallas as pl
from jax.experimental.pallas import tpu as pltpu

F32 = jnp.float32
BF16 = jnp.bfloat16

N_HEADS = 8
HEAD_DIM = 128
LRU_BLOCKS = 12
LRU_BLOCK = 128
CONV_WIDTH = 4
LRU_C = 8.0
N_MOD = 6
EPS = 1e-6
LOG2E = 1.4426950408889634
F32_MIN_NORMAL = 2.0 ** -126

V7X_VMEM_LIMIT_BYTES = 56 * 1024 * 1024
LANES = 128
SUBLANES = 8

ADA_TN = 1024
INPROJ_TM = 512
ATTN_TQ = 256
ATTN_TK = 256
ATTN_HP = 4
ATTN_RS = 32
ATTN_SKEW = 2
LRU_TS = 256
LRU_SEG = LRU_TS // SUBLANES
MERGE_TM = 512
FFN_TM = 512
FFN_CHUNK = 1408


def _params(*sem):
    return pltpu.CompilerParams(dimension_semantics=sem, vmem_limit_bytes=V7X_VMEM_LIMIT_BYTES)


def _resident(shape):
    nd = len(shape)
    return pl.BlockSpec(shape, lambda *_: (0,) * nd, pipeline_mode=pl.Buffered(1))


def _bdot(a, b):
    return jnp.dot(a, b, preferred_element_type=F32)


def _rms(x, g):
    ms = jnp.mean(x * x, axis=-1, keepdims=True)
    return x * lax.rsqrt(ms + EPS) * g


def _ada_kernel(c_ref, w_ref, b_ref, o_ref):
    c = c_ref[...]
    ca = c * jax.nn.sigmoid(c)
    o_ref[...] = jnp.dot(ca, w_ref[...], preferred_element_type=F32,
                         precision=lax.Precision.HIGHEST) + b_ref[...]


def _ada(c, w, b):
    bsz, d = c.shape
    n = w.shape[1]
    return pl.pallas_call(
        _ada_kernel,
        out_shape=jax.ShapeDtypeStruct((bsz, n), F32),
        grid=(n // ADA_TN,),
        in_specs=[pl.BlockSpec((bsz, d), lambda j: (0, 0)),
                  pl.BlockSpec((d, ADA_TN), lambda j: (0, j)),
                  pl.BlockSpec((1, ADA_TN), lambda j: (0, j))],
        out_specs=pl.BlockSpec((bsz, ADA_TN), lambda j: (0, j)),
        compiler_params=_params("arbitrary"),
        name="ada",
    )(c, w, b.reshape(1, n))


def _inproj_kernel(x_ref, mod_ref, g1_ref, w_ref, qg_ref, kg_ref,
                   q_ref, k_ref, v_ref, xr_ref, gr_ref, ga_ref, gb_ref, *, widths):
    x = x_ref[0]
    shift = mod_ref[0, 0:1, :]
    scale = mod_ref[0, 1:2, :]
    h = (_rms(x, g1_ref[...]) * (1.0 + scale) + shift).astype(BF16)

    offs = [0]
    for wd in widths:
        offs.append(offs[-1] + wd)

    def proj(idx):
        return _bdot(h, w_ref[:, offs[idx]:offs[idx + 1]])

    q_scale = HEAD_DIM ** -0.5 * LOG2E
    for idx, g_ref, o_ref, mul in ((0, qg_ref, q_ref, q_scale), (1, kg_ref, k_ref, 1.0)):
        p = proj(idx)
        for hh in range(N_HEADS):
            sl = slice(hh * HEAD_DIM, (hh + 1) * HEAD_DIM)
            ph = _rms(p[:, sl], g_ref[...])
            if mul != 1.0:
                ph = ph * mul
            o_ref[0, :, sl] = ph.astype(BF16)
    v_ref[0] = proj(2).astype(BF16)
    xr = proj(3)
    for n in range(LRU_BLOCKS):
        for c0 in range(0, xr.shape[0], LRU_TS):
            for sg in range(SUBLANES):
                t0 = c0 + sg * LRU_SEG
                xr_ref[0, n, pl.ds(c0 + sg, LRU_SEG, stride=SUBLANES), :] = (
                    xr[t0:t0 + LRU_SEG, n * LRU_BLOCK:(n + 1) * LRU_BLOCK])
    gr_ref[0] = jax.nn.gelu(proj(4)).astype(BF16)
    ga_ref[0] = jax.nn.sigmoid(proj(5)).astype(BF16)
    gb_ref[0] = jax.nn.sigmoid(proj(6)).astype(BF16)


def _inproj(x, mod, g1, w_in, qg, kg, widths):
    bsz, s, d = x.shape
    tm = INPROJ_TM
    tok = lambda wd: pl.BlockSpec((1, tm, wd), lambda b, i: (b, i, 0))
    out_dtypes = (BF16, BF16, BF16, F32, BF16, BF16, BF16)
    return pl.pallas_call(
        functools.partial(_inproj_kernel, widths=widths),
        out_shape=[jax.ShapeDtypeStruct((bsz, LRU_BLOCKS, s, LRU_BLOCK) if idx == 3 else (bsz, s, wd), dt)
                   for idx, (wd, dt) in enumerate(zip(widths, out_dtypes))],
        grid=(bsz, s // tm),
        in_specs=[tok(d),
                  pl.BlockSpec((1, N_MOD, d), lambda b, i: (b, 0, 0)),
                  _resident((1, d)),
                  _resident(w_in.shape),
                  _resident((1, HEAD_DIM)),
                  _resident((1, HEAD_DIM))],
        out_specs=[pl.BlockSpec((1, LRU_BLOCKS, tm, LRU_BLOCK), lambda b, i: (b, 0, i, 0)) if idx == 3
                   else tok(wd) for idx, wd in enumerate(widths)],
        compiler_params=_params("parallel", "arbitrary"),
        name="inproj",
    )(x, mod, g1, w_in, qg, kg)


def _attn_kernel(q_ref, k_ref, v_ref, tri_ref, o_ref, acc_ref, carry_ref, tot_ref, z_ref):
    tq, tk = ATTN_TQ, ATTN_TK
    i = pl.program_id(2)
    heads = range(ATTN_HP)
    lanes = [slice(hh * HEAD_DIM, (hh + 1) * HEAD_DIM) for hh in heads]
    strips = [slice(r0, r0 + ATTN_RS) for r0 in range(0, tq, ATTN_RS)]
    acc_ref[...] = jnp.zeros_like(acc_ref)
    carry_ref[...] = jnp.zeros_like(carry_ref)

    def scores(j):
        k0 = pl.multiple_of(j * tk, tk)
        for hh in heads:
            z_ref[hh] = lax.dot_general(q_ref[0, :, lanes[hh]], k_ref[0, pl.ds(k0, tk), lanes[hh]],
                                        (((1,), (1,)), ((), ())), preferred_element_type=F32)

    def block(j, diagonal):
        k0 = pl.multiple_of(j * tk, tk)
        if diagonal:
            row = lax.broadcasted_iota(jnp.int32, (tq, tk), 0)
            col = lax.broadcasted_iota(jnp.int32, (tq, tk), 1)
            causal = col < row
        log_beta, suffix = {}, {}

        def logs(hh):
            lbs, loms = [], []
            for rs in strips:
                zs = z_ref[hh, rs, :]
                sp = jnp.log(1.0 + jnp.exp2(-jnp.abs(zs))) * LOG2E
                lb = jnp.minimum(zs, 0.0) - sp
                log_om = lb - zs
                if diagonal:
                    log_om = jnp.where(causal[rs], log_om, 0.0)
                tot = jnp.sum(log_om, axis=1, keepdims=True)
                tot_ref[hh, rs, :] = jnp.broadcast_to(tot, (ATTN_RS, LANES))
                lbs.append(lb)
                loms.append(log_om.astype(BF16))
            suffix[hh] = _bdot(jnp.concatenate(loms, axis=0), tri_ref[...])
            log_beta[hh] = lbs

        def weights(hh):
            ws = []
            for si, rs in enumerate(strips):
                carry = carry_ref[hh, rs, :]
                e = log_beta[hh][si] + suffix[hh][rs] + jnp.concatenate([carry] * (tk // LANES), axis=1)
                w = jnp.exp2(e)
                if diagonal:
                    w = jnp.where(causal[rs], w, 0.0)
                ws.append(w.astype(BF16))
                carry_ref[hh, rs, :] = carry + tot_ref[hh, rs, :]
            acc_ref[hh] += _bdot(jnp.concatenate(ws, axis=0), v_ref[0, pl.ds(k0, tk), lanes[hh]])

        for step in range(ATTN_HP + ATTN_SKEW):
            if step < ATTN_HP:
                logs(step)
            if step == ATTN_HP - 1:
                scores(jnp.maximum(j - 1, 0))
            if step >= ATTN_SKEW:
                weights(step - ATTN_SKEW)

    scores(i)
    block(i, True)

    def body(it, _):
        block(i - 1 - it, False)
        return 0

    lax.fori_loop(0, i, body, 0)
    for hh in heads:
        o_ref[0, :, lanes[hh]] = acc_ref[hh].astype(o_ref.dtype)


def _attn(q, k, v):
    bsz, s, width = q.shape
    tq, tk = ATTN_TQ, ATTN_TK
    gw = ATTN_HP * HEAD_DIM
    assert tq == tk and s % tq == 0 and width % gw == 0
    r = lax.broadcasted_iota(jnp.int32, (tk, tk), 0)
    cidx = lax.broadcasted_iota(jnp.int32, (tk, tk), 1)
    tri = (r > cidx).astype(BF16)
    return pl.pallas_call(
        _attn_kernel,
        out_shape=jax.ShapeDtypeStruct((bsz, s, width), BF16),
        grid=(bsz, width // gw, s // tq),
        in_specs=[pl.BlockSpec((1, tq, gw), lambda b, h, i: (b, i, h)),
                  pl.BlockSpec((1, s, gw), lambda b, h, i: (b, 0, h)),
                  pl.BlockSpec((1, s, gw), lambda b, h, i: (b, 0, h)),
                  _resident((tk, tk))],
        out_specs=pl.BlockSpec((1, tq, gw), lambda b, h, i: (b, i, h)),
        scratch_shapes=[pltpu.VMEM((ATTN_HP, tq, HEAD_DIM), F32),
                        pltpu.VMEM((ATTN_HP, tq, LANES), F32),
                        pltpu.VMEM((ATTN_HP, tq, LANES), F32),
                        pltpu.VMEM((ATTN_HP, tq, tk), F32)],
        compiler_params=_params("parallel", "parallel", "arbitrary"),
        name="attn",
    )(q, k, v, tri)


def _sublane_scan(a, b, sub):
    for dist in (1, 2, 4):
        inside = sub >= dist
        a_prev = jnp.where(inside, pltpu.roll(a, dist, 0), 1.0)
        b_prev = jnp.where(inside, pltpu.roll(b, dist, 0), 0.0)
        b = a * b_prev + b
        a = a * a_prev
    return b


def _lru_kernel(xr_ref, gate_ref, cw_ref, cb_ref, wg_ref, bg_ref, lam_ref, y_ref,
                tail_ref, hlast_ref, h_s):
    ts, seg = LRU_TS, LRU_SEG
    hist = CONV_WIDTH - 1

    @pl.when(pl.program_id(1) == 0)
    def _():
        tail_ref[...] = jnp.zeros_like(tail_ref)
        hlast_ref[...] = jnp.zeros_like(hlast_ref)

    sub = lax.broadcasted_iota(jnp.int32, (SUBLANES, LRU_BLOCK), 0)
    sub_hist = lax.broadcasted_iota(jnp.int32, (hist, SUBLANES, LRU_BLOCK), 1)

    for n in range(LRU_BLOCKS):
        sl = slice(n * LRU_BLOCK, (n + 1) * LRU_BLOCK)
        x = xr_ref[0, n]
        rolled = pltpu.roll(x[ts - hist * SUBLANES:].reshape(hist, SUBLANES, LRU_BLOCK), 1, 1)
        before = jnp.where(sub_hist == 0, tail_ref[n], rolled)
        tail_ref[n] = rolled
        xe = jnp.concatenate([before.reshape(hist * SUBLANES, LRU_BLOCK), x], axis=0)
        xc = cb_ref[:, sl]
        for kk in range(CONV_WIDTH):
            xc = xc + xe[kk * SUBLANES:kk * SUBLANES + ts] * cw_ref[kk:kk + 1, sl]
        th = jnp.tanh(_bdot(xc.astype(BF16), wg_ref[n]) + bg_ref[n])
        ig = 0.5 * th[:, LRU_BLOCK:] + 0.5
        neg_lam = -lam_ref[:, sl]
        softplus = jnp.maximum(neg_lam, 0.0) + jnp.log(1.0 + jnp.exp(-jnp.abs(neg_lam)))
        half_c = (-0.5 * LRU_C) * softplus
        log_a = half_c * th[:, :LRU_BLOCK] + half_c
        a = jnp.exp(log_a)
        m = -jnp.tanh(log_a) * (1.0 + a * a)
        mult = m * lax.rsqrt(jnp.maximum(m, F32_MIN_NORMAL))
        u = mult * (ig * xc)
        local, decay = [u[:SUBLANES]], [a[:SUBLANES]]
        for r in range(1, seg):
            rows = slice(r * SUBLANES, (r + 1) * SUBLANES)
            local.append(a[rows] * local[-1] + u[rows])
            decay.append(a[rows] * decay[-1])
        chunk_entry = jnp.broadcast_to(hlast_ref[n, SUBLANES - 1:SUBLANES, :], (SUBLANES, LRU_BLOCK))
        first = sub == 0
        entry = _sublane_scan(jnp.where(first, 0.0, pltpu.roll(decay[-1], 1, 0)),
                              jnp.where(first, chunk_entry, pltpu.roll(local[-1], 1, 0)), sub)
        hlast_ref[n] = local[-1] + decay[-1] * entry
        for r in range(seg):
            h_s[n, r * SUBLANES:(r + 1) * SUBLANES, :] = local[r] + decay[r] * entry
        h = jnp.concatenate(
            [h_s[n, pl.ds(r0 * SUBLANES + sg, SUBLANES, stride=SUBLANES), :]
             for sg in range(SUBLANES) for r0 in range(0, seg, SUBLANES)], axis=0)
        y_ref[0, :, sl] = (gate_ref[0, :, sl].astype(F32) * h).astype(y_ref.dtype)


def _lru(xr, gate, conv_w, conv_b, w_gates, b_gates, lam):
    bsz, nblk, s, blk = xr.shape
    ts = LRU_TS
    tok = pl.BlockSpec((1, ts, nblk * blk), lambda b, i: (b, i, 0))
    return pl.pallas_call(
        _lru_kernel,
        out_shape=jax.ShapeDtypeStruct((bsz, s, nblk * blk), BF16),
        grid=(bsz, s // ts),
        in_specs=[pl.BlockSpec((1, nblk, ts, blk), lambda b, i: (b, 0, i, 0)), tok,
                  _resident(conv_w.shape), _resident(conv_b.shape),
                  _resident(w_gates.shape), _resident(b_gates.shape), _resident(lam.shape)],
        out_specs=tok,
        scratch_shapes=[pltpu.VMEM((nblk, CONV_WIDTH - 1, SUBLANES, blk), F32),
                        pltpu.VMEM((nblk, SUBLANES, blk), F32),
                        pltpu.VMEM((nblk, ts, blk), F32)],
        compiler_params=_params("parallel", "arbitrary"),
        name="lru",
    )(xr, gate, conv_w, conv_b, w_gates, b_gates, lam)


def _merge_kernel(o_ref, y_ref, ga_ref, gb_ref, x_ref, mod_ref, wpa_ref, wpl_ref, wo_ref, x1_ref):
    p_attn = _bdot(o_ref[0], wpa_ref[...])
    p_lru = _bdot(y_ref[0], wpl_ref[...])
    merged = ga_ref[0].astype(F32) * p_attn + gb_ref[0].astype(F32) * p_lru
    mix = _bdot(merged.astype(BF16), wo_ref[...])
    gate1 = mod_ref[0, 2:3, :]
    x1_ref[0] = x_ref[0] + gate1 * mix


def _merge(o, y, ga, gb, x, mod, wpa, wpl, wo):
    bsz, s, d = x.shape
    tm = MERGE_TM
    tok = lambda wd: pl.BlockSpec((1, tm, wd), lambda b, i: (b, i, 0))
    return pl.pallas_call(
        _merge_kernel,
        out_shape=jax.ShapeDtypeStruct((bsz, s, d), F32),
        grid=(bsz, s // tm),
        in_specs=[tok(o.shape[2]), tok(y.shape[2]), tok(d), tok(d), tok(d),
                  pl.BlockSpec((1, N_MOD, d), lambda b, i: (b, 0, 0)),
                  _resident(wpa.shape), _resident(wpl.shape), _resident(wo.shape)],
        out_specs=tok(d),
        compiler_params=_params("parallel", "arbitrary"),
        name="merge",
    )(o, y, ga, gb, x, mod, wpa, wpl, wo)


def _ffn_kernel(x_ref, mod_ref, g2_ref, wg_ref, wu_ref, wd_ref, out_ref):
    x = x_ref[0]
    shift = mod_ref[0, 3:4, :]
    scale = mod_ref[0, 4:5, :]
    gate2 = mod_ref[0, 5:6, :]
    h = (_rms(x, g2_ref[...]) * (1.0 + scale) + shift).astype(BF16)
    hidden = wg_ref.shape[1]
    acc = jnp.zeros(x.shape, F32)
    for c0 in range(0, hidden, FFN_CHUNK):
        g = _bdot(h, wg_ref[:, c0:c0 + FFN_CHUNK])
        u = _bdot(h, wu_ref[:, c0:c0 + FFN_CHUNK])
        act = (g * jax.nn.sigmoid(g) * u).astype(BF16)
        acc = acc + _bdot(act, wd_ref[c0:c0 + FFN_CHUNK, :])
    out_ref[0] = x + gate2 * acc


def _ffn(x1, mod, g2, wg, wu, wd):
    bsz, s, d = x1.shape
    tm = FFN_TM
    assert wg.shape[1] % FFN_CHUNK == 0
    tok = pl.BlockSpec((1, tm, d), lambda b, i: (b, i, 0))
    return pl.pallas_call(
        _ffn_kernel,
        out_shape=jax.ShapeDtypeStruct((bsz, s, d), F32),
        grid=(bsz, s // tm),
        in_specs=[tok,
                  pl.BlockSpec((1, N_MOD, d), lambda b, i: (b, 0, 0)),
                  _resident((1, d)),
                  _resident(wg.shape), _resident(wu.shape), _resident(wd.shape)],
        out_specs=tok,
        compiler_params=_params("parallel", "arbitrary"),
        name="ffn",
    )(x1, mod, g2, wg, wu, wd)


def kernel(x, c, w_ada, b_ada, norm1_g, w_in, q_norm_g, k_norm_g, conv_w, conv_b,
           w_rg, b_rg, w_ig, b_ig, lru_lambda, w_proj_attn, w_proj_lru, w_out,
           norm2_g, w_ffn_in, w_ffn_out):
    bsz, s, d = x.shape
    depth = w_ada.shape[0]
    attn_w = N_HEADS * HEAD_DIM
    lru_w = LRU_BLOCKS * LRU_BLOCK
    widths = (attn_w, attn_w, attn_w, lru_w, lru_w, d, d)
    hidden = w_ffn_out.shape[1]
    for l in range(depth):
        mod = _ada(c, w_ada[l], b_ada[l]).reshape(bsz, N_MOD, d)
        q, k, v, xr, gate, ga, gb = _inproj(
            x, mod, norm1_g[l].reshape(1, d), w_in[l].astype(BF16),
            q_norm_g[l].reshape(1, HEAD_DIM), k_norm_g[l].reshape(1, HEAD_DIM), widths)
        o = _attn(q, k, v)
        w_gates = (0.5 * jnp.concatenate([w_rg[l], w_ig[l]], axis=-1)).astype(BF16)
        b_gates = 0.5 * jnp.concatenate([b_rg[l].reshape(LRU_BLOCKS, 1, LRU_BLOCK),
                                         b_ig[l].reshape(LRU_BLOCKS, 1, LRU_BLOCK)], axis=-1)
        y = _lru(xr, gate, conv_w[l], conv_b[l].reshape(1, lru_w), w_gates, b_gates,
                 lru_lambda[l].reshape(1, lru_w))
        x1 = _merge(o, y, ga, gb, x, mod, w_proj_attn[l].astype(BF16),
                    w_proj_lru[l].astype(BF16), w_out[l].astype(BF16))
        w_ffn = w_ffn_in[l].astype(BF16)
        x = _ffn(x1, mod, norm2_g[l].reshape(1, d), w_ffn[:, :hidden], w_ffn[:, hidden:],
                 w_ffn_out[l].astype(BF16))
    return x
```

```python
import functools

import jax
import jax.numpy as jnp
from jax import lax
from jax.experimental import pallas as pl
from jax.experimental.pallas import tpu as pltpu

F32 = jnp.float32
BF16 = jnp.bfloat16

N_HEADS = 8
HEAD_DIM = 128
LRU_BLOCKS = 12
LRU_BLOCK = 128
CONV_WIDTH = 4
LRU_C = 8.0
N_MOD = 6
EPS = 1e-6
LOG2E = 1.4426950408889634
F32_MIN_NORMAL = 2.0 ** -126

V7X_VMEM_LIMIT_BYTES = 56 * 1024 * 1024
LANES = 128
SUBLANES = 8

ADA_TN = 1024
INPROJ_TM = 512
ATTN_TQ = 256
ATTN_TK = 256
ATTN_HP = 4
ATTN_RS = 32
ATTN_SPLIT = 1
ATTN_SKEW = 3
LRU_TS = 256
LRU_SEG = LRU_TS // SUBLANES
MERGE_TM = 512
FFN_TM = 512
FFN_CHUNK = 1408


def _params(*sem):
    return pltpu.CompilerParams(dimension_semantics=sem, vmem_limit_bytes=V7X_VMEM_LIMIT_BYTES)


def _resident(shape):
    nd = len(shape)
    return pl.BlockSpec(shape, lambda *_: (0,) * nd, pipeline_mode=pl.Buffered(1))


def _bdot(a, b):
    return jnp.dot(a, b, preferred_element_type=F32)


def _rms(x, g):
    ms = jnp.mean(x * x, axis=-1, keepdims=True)
    return x * lax.rsqrt(ms + EPS) * g


def _ada_kernel(c_ref, w_ref, b_ref, o_ref):
    c = c_ref[...]
    ca = c * jax.nn.sigmoid(c)
    o_ref[...] = jnp.dot(ca, w_ref[...], preferred_element_type=F32,
                         precision=lax.Precision.HIGHEST) + b_ref[...]


def _ada(c, w, b):
    bsz, d = c.shape
    n = w.shape[1]
    return pl.pallas_call(
        _ada_kernel,
        out_shape=jax.ShapeDtypeStruct((bsz, n), F32),
        grid=(n // ADA_TN,),
        in_specs=[pl.BlockSpec((bsz, d), lambda j: (0, 0)),
                  pl.BlockSpec((d, ADA_TN), lambda j: (0, j)),
                  pl.BlockSpec((1, ADA_TN), lambda j: (0, j))],
        out_specs=pl.BlockSpec((bsz, ADA_TN), lambda j: (0, j)),
        compiler_params=_params("arbitrary"),
        name="ada",
    )(c, w, b.reshape(1, n))


def _inproj_kernel(x_ref, mod_ref, g1_ref, w_ref, qg_ref, kg_ref,
                   q_ref, k_ref, v_ref, xr_ref, gr_ref, ga_ref, gb_ref, *, widths):
    x = x_ref[0]
    shift = mod_ref[0, 0:1, :]
    scale = mod_ref[0, 1:2, :]
    h = (_rms(x, g1_ref[...]) * (1.0 + scale) + shift).astype(BF16)

    offs = [0]
    for wd in widths:
        offs.append(offs[-1] + wd)

    def proj(idx):
        return _bdot(h, w_ref[:, offs[idx]:offs[idx + 1]])

    gr_ref[0] = jax.nn.gelu(proj(4)).astype(BF16)
    ga_ref[0] = (0.5 * jnp.tanh(0.5 * proj(5)) + 0.5).astype(BF16)
    gb_ref[0] = (0.5 * jnp.tanh(0.5 * proj(6)) + 0.5).astype(BF16)
    q_scale = HEAD_DIM ** -0.5 * LOG2E
    for idx, g_ref, o_ref, mul in ((0, qg_ref, q_ref, q_scale), (1, kg_ref, k_ref, 1.0)):
        p = proj(idx)
        for hh in range(N_HEADS):
            sl = slice(hh * HEAD_DIM, (hh + 1) * HEAD_DIM)
            ph = _rms(p[:, sl], g_ref[...])
            if mul != 1.0:
                ph = ph * mul
            o_ref[0, :, sl] = ph.astype(BF16)
    v_ref[0] = proj(2).astype(BF16)
    xr = proj(3)
    for n in range(LRU_BLOCKS):
        for c0 in range(0, xr.shape[0], LRU_TS):
            for sg in range(SUBLANES):
                t0 = c0 + sg * LRU_SEG
                xr_ref[0, n, pl.ds(c0 + sg, LRU_SEG, stride=SUBLANES), :] = (
                    xr[t0:t0 + LRU_SEG, n * LRU_BLOCK:(n + 1) * LRU_BLOCK])


def _inproj(x, mod, g1, w_in, qg, kg, widths):
    bsz, s, d = x.shape
    tm = INPROJ_TM
    tok = lambda wd: pl.BlockSpec((1, tm, wd), lambda b, i: (b, i, 0))
    out_dtypes = (BF16, BF16, BF16, F32, BF16, BF16, BF16)
    return pl.pallas_call(
        functools.partial(_inproj_kernel, widths=widths),
        out_shape=[jax.ShapeDtypeStruct((bsz, LRU_BLOCKS, s, LRU_BLOCK) if idx == 3 else (bsz, s, wd), dt)
                   for idx, (wd, dt) in enumerate(zip(widths, out_dtypes))],
        grid=(bsz, s // tm),
        in_specs=[tok(d),
                  pl.BlockSpec((1, N_MOD, d), lambda b, i: (b, 0, 0)),
                  _resident((1, d)),
                  _resident(w_in.shape),
                  _resident((1, HEAD_DIM)),
                  _resident((1, HEAD_DIM))],
        out_specs=[pl.BlockSpec((1, LRU_BLOCKS, tm, LRU_BLOCK), lambda b, i: (b, 0, i, 0)) if idx == 3
                   else tok(wd) for idx, wd in enumerate(widths)],
        compiler_params=_params("parallel", "arbitrary"),
        name="inproj",
    )(x, mod, g1, w_in, qg, kg)


def _attn_kernel(q_ref, k_ref, v_ref, tri_ref, o_ref, acc_ref, carry_ref, tot_ref, z_ref):
    tq, tk = ATTN_TQ, ATTN_TK
    i = pl.program_id(2)
    heads = range(ATTN_HP)
    lanes = [slice(hh * HEAD_DIM, (hh + 1) * HEAD_DIM) for hh in heads]
    strips = [slice(r0, r0 + ATTN_RS) for r0 in range(0, tq, ATTN_RS)]
    acc_ref[...] = jnp.zeros_like(acc_ref)
    carry_ref[...] = jnp.zeros_like(carry_ref)

    def scores(qi, kj, which=heads):
        q0 = pl.multiple_of(qi * tq, tq)
        k0 = pl.multiple_of(kj * tk, tk)
        for hh in which:
            z_ref[hh] = lax.dot_general(
                q_ref[0, pl.ds(q0, tq), lanes[hh]], k_ref[0, pl.ds(k0, tk), lanes[hh]],
                (((1,), (1,)), ((), ())), preferred_element_type=F32)

    def scores_after(j, hh):
        last = j == 0
        nxt = jnp.minimum(i + 1, pl.num_programs(2) - 1)
        scores(jnp.where(last, nxt, i), jnp.where(last, nxt, j - 1), (hh,))

    def block(j, diagonal):
        k0 = pl.multiple_of(j * tk, tk)
        if diagonal:
            row = lax.broadcasted_iota(jnp.int32, (tq, tk), 0)
            col = lax.broadcasted_iota(jnp.int32, (tq, tk), 1)
            causal = col < row
        log_beta, suffix = {}, {}

        per_part = len(strips) // ATTN_SPLIT
        chains = [(hh, strips[p * per_part:(p + 1) * per_part])
                  for hh in heads for p in range(ATTN_SPLIT)]

        def logs(c):
            hh, part = chains[c]
            lbs, loms = [], []
            for rs in part:
                zs = z_ref[hh, rs, :]
                sp = jnp.log(1.0 + jnp.exp2(-jnp.abs(zs))) * LOG2E
                lb = jnp.minimum(zs, 0.0) - sp
                log_om = lb - zs
                if diagonal:
                    log_om = jnp.where(causal[rs], log_om, 0.0)
                tot = jnp.sum(log_om, axis=1, keepdims=True)
                tot_ref[hh, rs, :] = jnp.broadcast_to(tot, (ATTN_RS, LANES))
                lbs.append(lb)
                loms.append(log_om.astype(BF16))
            suffix[c] = _bdot(jnp.concatenate(loms, axis=0), tri_ref[...])
            log_beta[c] = lbs

        def weights(c):
            hh, part = chains[c]
            ws = []
            for si, rs in enumerate(part):
                carry = carry_ref[hh, rs, :]
                sfx = suffix[c][si * ATTN_RS:(si + 1) * ATTN_RS]
                e = log_beta[c][si] + sfx + jnp.concatenate([carry] * (tk // LANES), axis=1)
                w = jnp.exp2(e)
                if diagonal:
                    w = jnp.where(causal[rs], w, 0.0)
                ws.append(w.astype(BF16))
                carry_ref[hh, rs, :] = carry + tot_ref[hh, rs, :]
            rows = slice(part[0].start, part[-1].stop)
            acc_ref[hh, rows, :] += _bdot(jnp.concatenate(ws, axis=0),
                                          v_ref[0, pl.ds(k0, tk), lanes[hh]])

        for step in range(len(chains) + ATTN_SKEW):
            if step < len(chains):
                logs(step)
                if (step + 1) % ATTN_SPLIT == 0:
                    scores_after(j, chains[step][0])
            if step >= ATTN_SKEW:
                weights(step - ATTN_SKEW)

    @pl.when(i == 0)
    def _():
        scores(i, i)

    block(i, True)

    def body(it, _):
        block(i - 1 - it, False)
        return 0

    lax.fori_loop(0, i, body, 0)
    for hh in heads:
        o_ref[0, :, lanes[hh]] = acc_ref[hh].astype(o_ref.dtype)


def _attn(q, k, v):
    bsz, s, width = q.shape
    tq, tk = ATTN_TQ, ATTN_TK
    gw = ATTN_HP * HEAD_DIM
    assert tq == tk and s % tq == 0 and width % gw == 0
    r = lax.broadcasted_iota(jnp.int32, (tk, tk), 0)
    cidx = lax.broadcasted_iota(jnp.int32, (tk, tk), 1)
    tri = (r > cidx).astype(BF16)
    return pl.pallas_call(
        _attn_kernel,
        out_shape=jax.ShapeDtypeStruct((bsz, s, width), BF16),
        grid=(bsz, width // gw, s // tq),
        in_specs=[pl.BlockSpec((1, s, gw), lambda b, h, i: (b, 0, h)),
                  pl.BlockSpec((1, s, gw), lambda b, h, i: (b, 0, h)),
                  pl.BlockSpec((1, s, gw), lambda b, h, i: (b, 0, h)),
                  _resident((tk, tk))],
        out_specs=pl.BlockSpec((1, tq, gw), lambda b, h, i: (b, i, h)),
        scratch_shapes=[pltpu.VMEM((ATTN_HP, tq, HEAD_DIM), F32),
                        pltpu.VMEM((ATTN_HP, tq, LANES), F32),
                        pltpu.VMEM((ATTN_HP, tq, LANES), F32),
                        pltpu.VMEM((ATTN_HP, tq, tk), F32)],
        compiler_params=_params("parallel", "parallel", "arbitrary"),
        name="attn",
    )(q, k, v, tri)


def _sublane_scan(a, b, sub):
    for dist in (1, 2, 4):
        inside = sub >= dist
        a_prev = jnp.where(inside, pltpu.roll(a, dist, 0), 1.0)
        b_prev = jnp.where(inside, pltpu.roll(b, dist, 0), 0.0)
        b = a * b_prev + b
        a = a * a_prev
    return b


def _lru_kernel(xr_ref, gate_ref, cw_ref, cb_ref, wg_ref, bg_ref, lam_ref, y_ref,
                tail_ref, hlast_ref, h_s):
    ts, seg = LRU_TS, LRU_SEG
    hist = CONV_WIDTH - 1

    @pl.when(pl.program_id(1) == 0)
    def _():
        tail_ref[...] = jnp.zeros_like(tail_ref)
        hlast_ref[...] = jnp.zeros_like(hlast_ref)

    sub = lax.broadcasted_iota(jnp.int32, (SUBLANES, LRU_BLOCK), 0)
    sub_hist = lax.broadcasted_iota(jnp.int32, (hist, SUBLANES, LRU_BLOCK), 1)

    for n in range(LRU_BLOCKS):
        sl = slice(n * LRU_BLOCK, (n + 1) * LRU_BLOCK)
        x = xr_ref[0, n]
        rolled = pltpu.roll(x[ts - hist * SUBLANES:].reshape(hist, SUBLANES, LRU_BLOCK), 1, 1)
        before = jnp.where(sub_hist == 0, tail_ref[n], rolled)
        tail_ref[n] = rolled
        xe = jnp.concatenate([before.reshape(hist * SUBLANES, LRU_BLOCK), x], axis=0)
        xc = cb_ref[:, sl]
        for kk in range(CONV_WIDTH):
            xc = xc + xe[kk * SUBLANES:kk * SUBLANES + ts] * cw_ref[kk:kk + 1, sl]
        th = jnp.tanh(_bdot(xc.astype(BF16), wg_ref[n]) + bg_ref[n])
        ig = 0.5 * th[:, LRU_BLOCK:] + 0.5
        neg_lam = -lam_ref[:, sl]
        softplus = jnp.maximum(neg_lam, 0.0) + jnp.log(1.0 + jnp.exp(-jnp.abs(neg_lam)))
        half_c = (-0.5 * LRU_C) * softplus
        log_a = half_c * th[:, :LRU_BLOCK] + half_c
        a = jnp.exp(log_a)
        m = -jnp.tanh(log_a) * (1.0 + a * a)
        mult = m * lax.rsqrt(jnp.maximum(m, F32_MIN_NORMAL))
        u = mult * (ig * xc)
        local, decay = [u[:SUBLANES]], [a[:SUBLANES]]
        for r in range(1, seg):
            rows = slice(r * SUBLANES, (r + 1) * SUBLANES)
            local.append(a[rows] * local[-1] + u[rows])
            decay.append(a[rows] * decay[-1])
        chunk_entry = jnp.broadcast_to(hlast_ref[n, SUBLANES - 1:SUBLANES, :], (SUBLANES, LRU_BLOCK))
        first = sub == 0
        entry = _sublane_scan(jnp.where(first, 0.0, pltpu.roll(decay[-1], 1, 0)),
                              jnp.where(first, chunk_entry, pltpu.roll(local[-1], 1, 0)), sub)
        hlast_ref[n] = local[-1] + decay[-1] * entry
        for r in range(seg):
            h_s[n, r * SUBLANES:(r + 1) * SUBLANES, :] = local[r] + decay[r] * entry
        h = jnp.concatenate(
            [h_s[n, pl.ds(r0 * SUBLANES + sg, SUBLANES, stride=SUBLANES), :]
             for sg in range(SUBLANES) for r0 in range(0, seg, SUBLANES)], axis=0)
        y_ref[0, :, sl] = (gate_ref[0, :, sl].astype(F32) * h).astype(y_ref.dtype)


def _lru(xr, gate, conv_w, conv_b, w_gates, b_gates, lam):
    bsz, nblk, s, blk = xr.shape
    ts = LRU_TS
    tok = pl.BlockSpec((1, ts, nblk * blk), lambda b, i: (b, i, 0))
    return pl.pallas_call(
        _lru_kernel,
        out_shape=jax.ShapeDtypeStruct((bsz, s, nblk * blk), BF16),
        grid=(bsz, s // ts),
        in_specs=[pl.BlockSpec((1, nblk, ts, blk), lambda b, i: (b, 0, i, 0)), tok,
                  _resident(conv_w.shape), _resident(conv_b.shape),
                  _resident(w_gates.shape), _resident(b_gates.shape), _resident(lam.shape)],
        out_specs=tok,
        scratch_shapes=[pltpu.VMEM((nblk, CONV_WIDTH - 1, SUBLANES, blk), F32),
                        pltpu.VMEM((nblk, SUBLANES, blk), F32),
                        pltpu.VMEM((nblk, ts, blk), F32)],
        compiler_params=_params("parallel", "arbitrary"),
        name="lru",
    )(xr, gate, conv_w, conv_b, w_gates, b_gates, lam)


def _merge_kernel(o_ref, y_ref, ga_ref, gb_ref, x_ref, mod_ref, wpa_ref, wpl_ref, wo_ref, x1_ref):
    p_attn = _bdot(o_ref[0], wpa_ref[...])
    p_lru = _bdot(y_ref[0], wpl_ref[...])
    merged = ga_ref[0].astype(F32) * p_attn + gb_ref[0].astype(F32) * p_lru
    mix = _bdot(merged.astype(BF16), wo_ref[...])
    gate1 = mod_ref[0, 2:3, :]
    x1_ref[0] = x_ref[0] + gate1 * mix


def _merge(o, y, ga, gb, x, mod, wpa, wpl, wo):
    bsz, s, d = x.shape
    tm = MERGE_TM
    tok = lambda wd: pl.BlockSpec((1, tm, wd), lambda b, i: (b, i, 0))
    return pl.pallas_call(
        _merge_kernel,
        out_shape=jax.ShapeDtypeStruct((bsz, s, d), F32),
        grid=(bsz, s // tm),
        in_specs=[tok(o.shape[2]), tok(y.shape[2]), tok(d), tok(d), tok(d),
                  pl.BlockSpec((1, N_MOD, d), lambda b, i: (b, 0, 0)),
                  _resident(wpa.shape), _resident(wpl.shape), _resident(wo.shape)],
        out_specs=tok(d),
        compiler_params=_params("parallel", "arbitrary"),
        name="merge",
    )(o, y, ga, gb, x, mod, wpa, wpl, wo)


def _ffn_kernel(x_ref, mod_ref, g2_ref, wg_ref, wu_ref, wd_ref, out_ref):
    x = x_ref[0]
    shift = mod_ref[0, 3:4, :]
    scale = mod_ref[0, 4:5, :]
    gate2 = mod_ref[0, 5:6, :]
    h = (_rms(x, g2_ref[...]) * (1.0 + scale) + shift).astype(BF16)
    hidden = wg_ref.shape[1]
    acc = jnp.zeros(x.shape, F32)
    for c0 in range(0, hidden, FFN_CHUNK):
        g = _bdot(h, wg_ref[:, c0:c0 + FFN_CHUNK])
        u = _bdot(h, wu_ref[:, c0:c0 + FFN_CHUNK])
        half = 0.5 * g
        act = ((half + half * jnp.tanh(half)) * u).astype(BF16)
        acc = acc + _bdot(act, wd_ref[c0:c0 + FFN_CHUNK, :])
    out_ref[0] = x + gate2 * acc


def _ffn(x1, mod, g2, wg, wu, wd):
    bsz, s, d = x1.shape
    tm = FFN_TM
    assert wg.shape[1] % FFN_CHUNK == 0
    tok = pl.BlockSpec((1, tm, d), lambda b, i: (b, i, 0))
    return pl.pallas_call(
        _ffn_kernel,
        out_shape=jax.ShapeDtypeStruct((bsz, s, d), F32),
        grid=(bsz, s // tm),
        in_specs=[tok,
                  pl.BlockSpec((1, N_MOD, d), lambda b, i: (b, 0, 0)),
                  _resident((1, d)),
                  _resident(wg.shape), _resident(wu.shape), _resident(wd.shape)],
        out_specs=tok,
        compiler_params=_params("parallel", "arbitrary"),
        name="ffn",
    )(x1, mod, g2, wg, wu, wd)


def kernel(x, c, w_ada, b_ada, norm1_g, w_in, q_norm_g, k_norm_g, conv_w, conv_b,
           w_rg, b_rg, w_ig, b_ig, lru_lambda, w_proj_attn, w_proj_lru, w_out,
           norm2_g, w_ffn_in, w_ffn_out):
    bsz, s, d = x.shape
    depth = w_ada.shape[0]
    attn_w = N_HEADS * HEAD_DIM
    lru_w = LRU_BLOCKS * LRU_BLOCK
    widths = (attn_w, attn_w, attn_w, lru_w, lru_w, d, d)
    hidden = w_ffn_out.shape[1]
    for l in range(depth):
        mod = _ada(c, w_ada[l], b_ada[l]).reshape(bsz, N_MOD, d)
        q, k, v, xr, gate, ga, gb = _inproj(
            x, mod, norm1_g[l].reshape(1, d), w_in[l].astype(BF16),
            q_norm_g[l].reshape(1, HEAD_DIM), k_norm_g[l].reshape(1, HEAD_DIM), widths)
        o = _attn(q, k, v)
        w_gates = (0.5 * jnp.concatenate([w_rg[l], w_ig[l]], axis=-1)).astype(BF16)
        b_gates = 0.5 * jnp.concatenate([b_rg[l].reshape(LRU_BLOCKS, 1, LRU_BLOCK),
                                         b_ig[l].reshape(LRU_BLOCKS, 1, LRU_BLOCK)], axis=-1)
        y = _lru(xr, gate, conv_w[l], conv_b[l].reshape(1, lru_w), w_gates, b_gates,
                 lru_lambda[l].reshape(1, lru_w))
        x1 = _merge(o, y, ga, gb, x, mod, w_proj_attn[l].astype(BF16),
                    w_proj_lru[l].astype(BF16), w_out[l].astype(BF16))
        w_ffn = w_ffn_in[l].astype(BF16)
        x = _ffn(x1, mod, norm2_g[l].reshape(1, d), w_ffn[:, :hidden], w_ffn[:, hidden:],
                 w_ffn_out[l].astype(BF16))
    return x
```

```python
import functools

import jax
import jax.numpy as jnp
from jax import lax
from jax.experimental import pallas as pl
from jax.experimental.pallas import tpu as pltpu

F32 = jnp.float32
BF16 = jnp.bfloat16

N_HEADS = 8
HEAD_DIM = 128
LRU_BLOCKS = 12
LRU_BLOCK = 128
CONV_WIDTH = 4
LRU_C = 8.0
N_MOD = 6
EPS = 1e-6
LOG2E = 1.4426950408889634
F32_MIN_NORMAL = 2.0 ** -126

V7X_VMEM_LIMIT_BYTES = 56 * 1024 * 1024
LANES = 128
SUBLANES = 8

ADA_TN = 1024
INPROJ_TM = 512
ATTN_TQ = 256
ATTN_TK = 256
ATTN_HP = 4
ATTN_RS = 32
ATTN_SKEW = 3
LRU_TS = 256
LRU_SEG = LRU_TS // SUBLANES
MERGE_TM = 512
FFN_TM = 512
FFN_CHUNK = 1408


def _params(*sem):
    return pltpu.CompilerParams(dimension_semantics=sem, vmem_limit_bytes=V7X_VMEM_LIMIT_BYTES)


def _resident(shape):
    nd = len(shape)
    return pl.BlockSpec(shape, lambda *_: (0,) * nd, pipeline_mode=pl.Buffered(1))


def _bdot(a, b):
    return jnp.dot(a, b, preferred_element_type=F32)


def _rms(x, g):
    ms = jnp.mean(x * x, axis=-1, keepdims=True)
    return x * lax.rsqrt(ms + EPS) * g


def _ada_kernel(c_ref, w_ref, b_ref, o_ref):
    c = c_ref[...]
    ca = c * jax.nn.sigmoid(c)
    o_ref[...] = jnp.dot(ca, w_ref[...], preferred_element_type=F32,
                         precision=lax.Precision.HIGHEST) + b_ref[...]


def _ada(c, w, b):
    bsz, d = c.shape
    n = w.shape[1]
    return pl.pallas_call(
        _ada_kernel,
        out_shape=jax.ShapeDtypeStruct((bsz, n), F32),
        grid=(n // ADA_TN,),
        in_specs=[pl.BlockSpec((bsz, d), lambda j: (0, 0)),
                  pl.BlockSpec((d, ADA_TN), lambda j: (0, j)),
                  pl.BlockSpec((1, ADA_TN), lambda j: (0, j))],
        out_specs=pl.BlockSpec((bsz, ADA_TN), lambda j: (0, j)),
        compiler_params=_params("arbitrary"),
        name="ada",
    )(c, w, b.reshape(1, n))


def _inproj_kernel(x_ref, mod_ref, g1_ref, w_ref, qg_ref, kg_ref,
                   q_ref, k_ref, v_ref, xr_ref, gr_ref, ga_ref, gb_ref, *, widths):
    x = x_ref[0]
    shift = mod_ref[0, 0:1, :]
    scale = mod_ref[0, 1:2, :]
    h = (_rms(x, g1_ref[...]) * (1.0 + scale) + shift).astype(BF16)

    offs = [0]
    for wd in widths:
        offs.append(offs[-1] + wd)

    def proj(idx):
        return _bdot(h, w_ref[:, offs[idx]:offs[idx + 1]])

    gr_ref[0] = jax.nn.gelu(proj(4)).astype(BF16)
    ga_ref[0] = (0.5 * jnp.tanh(0.5 * proj(5)) + 0.5).astype(BF16)
    gb_ref[0] = (0.5 * jnp.tanh(0.5 * proj(6)) + 0.5).astype(BF16)
    q_scale = HEAD_DIM ** -0.5 * LOG2E
    for idx, g_ref, o_ref, mul in ((0, qg_ref, q_ref, q_scale), (1, kg_ref, k_ref, 1.0)):
        p = proj(idx)
        for hh in range(N_HEADS):
            sl = slice(hh * HEAD_DIM, (hh + 1) * HEAD_DIM)
            ph = _rms(p[:, sl], g_ref[...])
            if mul != 1.0:
                ph = ph * mul
            o_ref[0, :, sl] = ph.astype(BF16)
    v_ref[0] = proj(2).astype(BF16)
    xr = proj(3)
    for n in range(LRU_BLOCKS):
        for c0 in range(0, xr.shape[0], LRU_TS):
            for sg in range(SUBLANES):
                t0 = c0 + sg * LRU_SEG
                xr_ref[0, n, pl.ds(c0 + sg, LRU_SEG, stride=SUBLANES), :] = (
                    xr[t0:t0 + LRU_SEG, n * LRU_BLOCK:(n + 1) * LRU_BLOCK])


def _inproj(x, mod, g1, w_in, qg, kg, widths):
    bsz, s, d = x.shape
    tm = INPROJ_TM
    tok = lambda wd: pl.BlockSpec((1, tm, wd), lambda b, i: (b, i, 0))
    out_dtypes = (BF16, BF16, BF16, F32, BF16, BF16, BF16)
    return pl.pallas_call(
        functools.partial(_inproj_kernel, widths=widths),
        out_shape=[jax.ShapeDtypeStruct((bsz, LRU_BLOCKS, s, LRU_BLOCK) if idx == 3 else (bsz, s, wd), dt)
                   for idx, (wd, dt) in enumerate(zip(widths, out_dtypes))],
        grid=(bsz, s // tm),
        in_specs=[tok(d),
                  pl.BlockSpec((1, N_MOD, d), lambda b, i: (b, 0, 0)),
                  _resident((1, d)),
                  _resident(w_in.shape),
                  _resident((1, HEAD_DIM)),
                  _resident((1, HEAD_DIM))],
        out_specs=[pl.BlockSpec((1, LRU_BLOCKS, tm, LRU_BLOCK), lambda b, i: (b, 0, i, 0)) if idx == 3
                   else tok(wd) for idx, wd in enumerate(widths)],
        compiler_params=_params("parallel", "arbitrary"),
        name="inproj",
    )(x, mod, g1, w_in, qg, kg)


def _attn_kernel(q_ref, k_ref, v_ref, tri_ref, o_ref, acc_ref, carry_ref, tot_ref, z_ref, w_ref):
    tq, tk = ATTN_TQ, ATTN_TK
    i = pl.program_id(2)
    heads = range(ATTN_HP)
    lanes = [slice(hh * HEAD_DIM, (hh + 1) * HEAD_DIM) for hh in heads]
    strips = [slice(r0, r0 + ATTN_RS) for r0 in range(0, tq, ATTN_RS)]
    acc_ref[...] = jnp.zeros_like(acc_ref)
    carry_ref[...] = jnp.zeros_like(carry_ref)

    def scores(qi, kj, which=heads):
        q0 = pl.multiple_of(qi * tq, tq)
        k0 = pl.multiple_of(kj * tk, tk)
        for hh in which:
            z_ref[hh] = lax.dot_general(
                q_ref[0, pl.ds(q0, tq), lanes[hh]], k_ref[0, pl.ds(k0, tk), lanes[hh]],
                (((1,), (1,)), ((), ())), preferred_element_type=F32)

    def scores_after(j, hh):
        last = j == 0
        nxt = jnp.minimum(i + 1, pl.num_programs(2) - 1)
        scores(jnp.where(last, nxt, i), jnp.where(last, nxt, j - 1), (hh,))

    def values(kj, which=heads):
        k0 = pl.multiple_of(kj * tk, tk)
        for hh in which:
            acc_ref[hh] += _bdot(w_ref[hh], v_ref[0, pl.ds(k0, tk), lanes[hh]])

    def block(j, diagonal):
        if diagonal:
            row = lax.broadcasted_iota(jnp.int32, (tq, tk), 0)
            col = lax.broadcasted_iota(jnp.int32, (tq, tk), 1)
            causal = col < row
        log_beta, suffix = {}, {}

        def logs(hh):
            lbs, loms = [], []
            for rs in strips:
                zs = z_ref[hh, rs, :]
                sp = jnp.log(1.0 + jnp.exp2(-jnp.abs(zs))) * LOG2E
                lb = jnp.minimum(zs, 0.0) - sp
                log_om = lb - zs
                if diagonal:
                    log_om = jnp.where(causal[rs], log_om, 0.0)
                tot = jnp.sum(log_om, axis=1, keepdims=True)
                tot_ref[hh, rs, :] = jnp.broadcast_to(tot, (ATTN_RS, LANES))
                lbs.append(lb)
                loms.append(log_om.astype(BF16))
            suffix[hh] = _bdot(jnp.concatenate(loms, axis=0), tri_ref[...])
            log_beta[hh] = lbs

        def weights(hh):
            for si, rs in enumerate(strips):
                carry = carry_ref[hh, rs, :]
                e = log_beta[hh][si] + suffix[hh][rs] + jnp.concatenate([carry] * (tk // LANES), axis=1)
                w = jnp.exp2(e)
                if diagonal:
                    w = jnp.where(causal[rs], w, 0.0)
                w_ref[hh, rs, :] = w.astype(BF16)
                carry_ref[hh, rs, :] = carry + tot_ref[hh, rs, :]

        for step in range(ATTN_HP + ATTN_SKEW):
            if step < ATTN_HP:
                logs(step)
                scores_after(j, step)
                if not diagonal:
                    values(j + 1, (step,))
            if step >= ATTN_SKEW:
                weights(step - ATTN_SKEW)

    @pl.when(i == 0)
    def _():
        scores(i, i)

    block(i, True)

    def body(it, _):
        block(i - 1 - it, False)
        return 0

    lax.fori_loop(0, i, body, 0)
    values(0)
    for hh in heads:
        o_ref[0, :, lanes[hh]] = acc_ref[hh].astype(o_ref.dtype)


def _attn(q, k, v):
    bsz, s, width = q.shape
    tq, tk = ATTN_TQ, ATTN_TK
    gw = ATTN_HP * HEAD_DIM
    assert tq == tk and s % tq == 0 and width % gw == 0
    r = lax.broadcasted_iota(jnp.int32, (tk, tk), 0)
    cidx = lax.broadcasted_iota(jnp.int32, (tk, tk), 1)
    tri = (r > cidx).astype(BF16)
    return pl.pallas_call(
        _attn_kernel,
        out_shape=jax.ShapeDtypeStruct((bsz, s, width), BF16),
        grid=(bsz, width // gw, s // tq),
        in_specs=[pl.BlockSpec((1, s, gw), lambda b, h, i: (b, 0, h)),
                  pl.BlockSpec((1, s, gw), lambda b, h, i: (b, 0, h)),
                  pl.BlockSpec((1, s, gw), lambda b, h, i: (b, 0, h)),
                  _resident((tk, tk))],
        out_specs=pl.BlockSpec((1, tq, gw), lambda b, h, i: (b, i, h)),
        scratch_shapes=[pltpu.VMEM((ATTN_HP, tq, HEAD_DIM), F32),
                        pltpu.VMEM((ATTN_HP, tq, LANES), F32),
                        pltpu.VMEM((ATTN_HP, tq, LANES), F32),
                        pltpu.VMEM((ATTN_HP, tq, tk), F32),
                        pltpu.VMEM((ATTN_HP, tq, tk), BF16)],
        compiler_params=_params("parallel", "parallel", "arbitrary"),
        name="attn",
    )(q, k, v, tri)


def _sublane_scan(a, b, sub):
    for dist in (1, 2, 4):
        inside = sub >= dist
        a_prev = jnp.where(inside, pltpu.roll(a, dist, 0), 1.0)
        b_prev = jnp.where(inside, pltpu.roll(b, dist, 0), 0.0)
        b = a * b_prev + b
        a = a * a_prev
    return b


def _lru_kernel(xr_ref, gate_ref, cw_ref, cb_ref, wg_ref, bg_ref, lam_ref, y_ref,
                tail_ref, hlast_ref, h_s):
    ts, seg = LRU_TS, LRU_SEG
    hist = CONV_WIDTH - 1

    @pl.when(pl.program_id(1) == 0)
    def _():
        tail_ref[...] = jnp.zeros_like(tail_ref)
        hlast_ref[...] = jnp.zeros_like(hlast_ref)

    sub = lax.broadcasted_iota(jnp.int32, (SUBLANES, LRU_BLOCK), 0)
    sub_hist = lax.broadcasted_iota(jnp.int32, (hist, SUBLANES, LRU_BLOCK), 1)

    for n in range(LRU_BLOCKS):
        sl = slice(n * LRU_BLOCK, (n + 1) * LRU_BLOCK)
        x = xr_ref[0, n]
        rolled = pltpu.roll(x[ts - hist * SUBLANES:].reshape(hist, SUBLANES, LRU_BLOCK), 1, 1)
        before = jnp.where(sub_hist == 0, tail_ref[n], rolled)
        tail_ref[n] = rolled
        xe = jnp.concatenate([before.reshape(hist * SUBLANES, LRU_BLOCK), x], axis=0)
        xc = cb_ref[:, sl]
        for kk in range(CONV_WIDTH):
            xc = xc + xe[kk * SUBLANES:kk * SUBLANES + ts] * cw_ref[kk:kk + 1, sl]
        th = jnp.tanh(_bdot(xc.astype(BF16), wg_ref[n]) + bg_ref[n])
        ig = 0.5 * th[:, LRU_BLOCK:] + 0.5
        neg_lam = -lam_ref[:, sl]
        softplus = jnp.maximum(neg_lam, 0.0) + jnp.log(1.0 + jnp.exp(-jnp.abs(neg_lam)))
        half_c = (0.5 * LRU_C) * softplus
        nla = half_c * th[:, :LRU_BLOCK] + half_c
        a = jnp.exp2(nla * -LOG2E)
        m = jnp.tanh(nla) * (1.0 + a * a)
        mult = m * lax.rsqrt(jnp.maximum(m, F32_MIN_NORMAL))
        u = mult * (ig * xc)
        local, decay = [u[:SUBLANES]], [a[:SUBLANES]]
        for r in range(1, seg):
            rows = slice(r * SUBLANES, (r + 1) * SUBLANES)
            local.append(a[rows] * local[-1] + u[rows])
            decay.append(a[rows] * decay[-1])
        chunk_entry = jnp.broadcast_to(hlast_ref[n, SUBLANES - 1:SUBLANES, :], (SUBLANES, LRU_BLOCK))
        first = sub == 0
        entry = _sublane_scan(jnp.where(first, 0.0, pltpu.roll(decay[-1], 1, 0)),
                              jnp.where(first, chunk_entry, pltpu.roll(local[-1], 1, 0)), sub)
        hlast_ref[n] = local[-1] + decay[-1] * entry
        for r in range(seg):
            h_s[n, r * SUBLANES:(r + 1) * SUBLANES, :] = local[r] + decay[r] * entry
        h = jnp.concatenate(
            [h_s[n, pl.ds(r0 * SUBLANES + sg, SUBLANES, stride=SUBLANES), :]
             for sg in range(SUBLANES) for r0 in range(0, seg, SUBLANES)], axis=0)
        y_ref[0, :, sl] = (gate_ref[0, :, sl].astype(F32) * h).astype(y_ref.dtype)


def _lru(xr, gate, conv_w, conv_b, w_gates, b_gates, lam):
    bsz, nblk, s, blk = xr.shape
    ts = LRU_TS
    tok = pl.BlockSpec((1, ts, nblk * blk), lambda b, i: (b, i, 0))
    return pl.pallas_call(
        _lru_kernel,
        out_shape=jax.ShapeDtypeStruct((bsz, s, nblk * blk), BF16),
        grid=(bsz, s // ts),
        in_specs=[pl.BlockSpec((1, nblk, ts, blk), lambda b, i: (b, 0, i, 0)), tok,
                  _resident(conv_w.shape), _resident(conv_b.shape),
                  _resident(w_gates.shape), _resident(b_gates.shape), _resident(lam.shape)],
        out_specs=tok,
        scratch_shapes=[pltpu.VMEM((nblk, CONV_WIDTH - 1, SUBLANES, blk), F32),
                        pltpu.VMEM((nblk, SUBLANES, blk), F32),
                        pltpu.VMEM((nblk, ts, blk), F32)],
        compiler_params=_params("parallel", "arbitrary"),
        name="lru",
    )(xr, gate, conv_w, conv_b, w_gates, b_gates, lam)


def _merge_kernel(o_ref, y_ref, ga_ref, gb_ref, x_ref, mod_ref, wpa_ref, wpl_ref, wo_ref, x1_ref):
    p_attn = _bdot(o_ref[0], wpa_ref[...])
    p_lru = _bdot(y_ref[0], wpl_ref[...])
    merged = ga_ref[0].astype(F32) * p_attn + gb_ref[0].astype(F32) * p_lru
    mix = _bdot(merged.astype(BF16), wo_ref[...])
    gate1 = mod_ref[0, 2:3, :]
    x1_ref[0] = x_ref[0] + gate1 * mix


def _merge(o, y, ga, gb, x, mod, wpa, wpl, wo):
    bsz, s, d = x.shape
    tm = MERGE_TM
    tok = lambda wd: pl.BlockSpec((1, tm, wd), lambda b, i: (b, i, 0))
    return pl.pallas_call(
        _merge_kernel,
        out_shape=jax.ShapeDtypeStruct((bsz, s, d), F32),
        grid=(bsz, s // tm),
        in_specs=[tok(o.shape[2]), tok(y.shape[2]), tok(d), tok(d), tok(d),
                  pl.BlockSpec((1, N_MOD, d), lambda b, i: (b, 0, 0)),
                  _resident(wpa.shape), _resident(wpl.shape), _resident(wo.shape)],
        out_specs=tok(d),
        compiler_params=_params("parallel", "arbitrary"),
        name="merge",
    )(o, y, ga, gb, x, mod, wpa, wpl, wo)


def _ffn_kernel(x_ref, mod_ref, g2_ref, wg_ref, wu_ref, wd_ref, out_ref):
    x = x_ref[0]
    shift = mod_ref[0, 3:4, :]
    scale = mod_ref[0, 4:5, :]
    gate2 = mod_ref[0, 5:6, :]
    h = (_rms(x, g2_ref[...]) * (1.0 + scale) + shift).astype(BF16)
    hidden = wg_ref.shape[1]
    acc = jnp.zeros(x.shape, F32)
    for c0 in range(0, hidden, FFN_CHUNK):
        g = _bdot(h, wg_ref[:, c0:c0 + FFN_CHUNK])
        u = _bdot(h, wu_ref[:, c0:c0 + FFN_CHUNK])
        half = 0.5 * g
        act = ((half + half * jnp.tanh(half)) * u).astype(BF16)
        acc = acc + _bdot(act, wd_ref[c0:c0 + FFN_CHUNK, :])
    out_ref[0] = x + gate2 * acc


def _ffn(x1, mod, g2, wg, wu, wd):
    bsz, s, d = x1.shape
    tm = FFN_TM
    assert wg.shape[1] % FFN_CHUNK == 0
    tok = pl.BlockSpec((1, tm, d), lambda b, i: (b, i, 0))
    return pl.pallas_call(
        _ffn_kernel,
        out_shape=jax.ShapeDtypeStruct((bsz, s, d), F32),
        grid=(bsz, s // tm),
        in_specs=[tok,
                  pl.BlockSpec((1, N_MOD, d), lambda b, i: (b, 0, 0)),
                  _resident((1, d)),
                  _resident(wg.shape), _resident(wu.shape), _resident(wd.shape)],
        out_specs=tok,
        compiler_params=_params("parallel", "arbitrary"),
        name="ffn",
    )(x1, mod, g2, wg, wu, wd)


def kernel(x, c, w_ada, b_ada, norm1_g, w_in, q_norm_g, k_norm_g, conv_w, conv_b,
           w_rg, b_rg, w_ig, b_ig, lru_lambda, w_proj_attn, w_proj_lru, w_out,
           norm2_g, w_ffn_in, w_ffn_out):
    bsz, s, d = x.shape
    depth = w_ada.shape[0]
    attn_w = N_HEADS * HEAD_DIM
    lru_w = LRU_BLOCKS * LRU_BLOCK
    widths = (attn_w, attn_w, attn_w, lru_w, lru_w, d, d)
    hidden = w_ffn_out.shape[1]
    for l in range(depth):
        mod = _ada(c, w_ada[l], b_ada[l]).reshape(bsz, N_MOD, d)
        q, k, v, xr, gate, ga, gb = _inproj(
            x, mod, norm1_g[l].reshape(1, d), w_in[l].astype(BF16),
            q_norm_g[l].reshape(1, HEAD_DIM), k_norm_g[l].reshape(1, HEAD_DIM), widths)
        o = _attn(q, k, v)
        w_gates = (0.5 * jnp.concatenate([w_rg[l], w_ig[l]], axis=-1)).astype(BF16)
        b_gates = 0.5 * jnp.concatenate([b_rg[l].reshape(LRU_BLOCKS, 1, LRU_BLOCK),
                                         b_ig[l].reshape(LRU_BLOCKS, 1, LRU_BLOCK)], axis=-1)
        y = _lru(xr, gate, conv_w[l], conv_b[l].reshape(1, lru_w), w_gates, b_gates,
                 lru_lambda[l].reshape(1, lru_w))
        x1 = _merge(o, y, ga, gb, x, mod, w_proj_attn[l].astype(BF16),
                    w_proj_lru[l].astype(BF16), w_out[l].astype(BF16))
        w_ffn = w_ffn_in[l].astype(BF16)
        x = _ffn(x1, mod, norm2_g[l].reshape(1, d), w_ffn[:, :hidden], w_ffn[:, hidden:],
                 w_ffn_out[l].astype(BF16))
    return x
```

```python
import functools

import jax
import jax.numpy as jnp
from jax import lax
from jax.experimental import pallas as pl
from jax.experimental.pallas import tpu as pltpu

F32 = jnp.float32
BF16 = jnp.bfloat16

N_HEADS = 8
HEAD_DIM = 128
LRU_BLOCKS = 12
LRU_BLOCK = 128
CONV_WIDTH = 4
LRU_C = 8.0
N_MOD = 6
EPS = 1e-6
LOG2E = 1.4426950408889634
F32_MIN_NORMAL = 2.0 ** -126

V7X_VMEM_LIMIT_BYTES = 56 * 1024 * 1024
LANES = 128
SUBLANES = 8

ADA_TN = 1024
INPROJ_TM = 512
ATTN_TQ = 256
ATTN_TK = 256
ATTN_HP = 4
ATTN_RS = 32
ATTN_SKEW = 3
LRU_TS = 256
LRU_SEG = LRU_TS // SUBLANES
MERGE_TM = 512
FFN_TM = 512
FFN_CHUNK = 1408


def _params(*sem):
    return pltpu.CompilerParams(dimension_semantics=sem, vmem_limit_bytes=V7X_VMEM_LIMIT_BYTES)


def _resident(shape):
    nd = len(shape)
    return pl.BlockSpec(shape, lambda *_: (0,) * nd, pipeline_mode=pl.Buffered(1))


def _bdot(a, b):
    return jnp.dot(a, b, preferred_element_type=F32)


def _rms(x, g):
    ms = jnp.mean(x * x, axis=-1, keepdims=True)
    return x * lax.rsqrt(ms + EPS) * g


def _ada_kernel(c_ref, w_ref, b_ref, o_ref):
    c = c_ref[...]
    ca = c * jax.nn.sigmoid(c)
    o_ref[...] = jnp.dot(ca, w_ref[...], preferred_element_type=F32,
                         precision=lax.Precision.HIGHEST) + b_ref[...]


def _ada(c, w, b):
    bsz, d = c.shape
    n = w.shape[1]
    return pl.pallas_call(
        _ada_kernel,
        out_shape=jax.ShapeDtypeStruct((bsz, n), F32),
        grid=(n // ADA_TN,),
        in_specs=[pl.BlockSpec((bsz, d), lambda j: (0, 0)),
                  pl.BlockSpec((d, ADA_TN), lambda j: (0, j)),
                  pl.BlockSpec((1, ADA_TN), lambda j: (0, j))],
        out_specs=pl.BlockSpec((bsz, ADA_TN), lambda j: (0, j)),
        compiler_params=_params("arbitrary"),
        name="ada",
    )(c, w, b.reshape(1, n))


def _inproj_kernel(x_ref, mod_ref, g1_ref, w_ref, qg_ref, kg_ref,
                   q_ref, k_ref, v_ref, xr_ref, gr_ref, ga_ref, gb_ref, *, widths):
    x = x_ref[0]
    shift = mod_ref[0, 0:1, :]
    scale = mod_ref[0, 1:2, :]
    h = (_rms(x, g1_ref[...]) * (1.0 + scale) + shift).astype(BF16)

    offs = [0]
    for wd in widths:
        offs.append(offs[-1] + wd)

    def proj(idx):
        return _bdot(h, w_ref[:, offs[idx]:offs[idx + 1]])

    gr_ref[0] = jax.nn.gelu(proj(4)).astype(BF16)
    ga_ref[0] = (0.5 * jnp.tanh(0.5 * proj(5)) + 0.5).astype(BF16)
    gb_ref[0] = (0.5 * jnp.tanh(0.5 * proj(6)) + 0.5).astype(BF16)
    q_scale = HEAD_DIM ** -0.5 * LOG2E
    for idx, g_ref, o_ref, mul in ((0, qg_ref, q_ref, q_scale), (1, kg_ref, k_ref, 1.0)):
        p = proj(idx)
        for hh in range(N_HEADS):
            sl = slice(hh * HEAD_DIM, (hh + 1) * HEAD_DIM)
            ph = _rms(p[:, sl], g_ref[...])
            if mul != 1.0:
                ph = ph * mul
            o_ref[0, :, sl] = ph.astype(BF16)
    v_ref[0] = proj(2).astype(BF16)
    xr = proj(3)
    for n in range(LRU_BLOCKS):
        for c0 in range(0, xr.shape[0], LRU_TS):
            for sg in range(SUBLANES):
                t0 = c0 + sg * LRU_SEG
                xr_ref[0, n, pl.ds(c0 + sg, LRU_SEG, stride=SUBLANES), :] = (
                    xr[t0:t0 + LRU_SEG, n * LRU_BLOCK:(n + 1) * LRU_BLOCK])


def _inproj(x, mod, g1, w_in, qg, kg, widths):
    bsz, s, d = x.shape
    tm = INPROJ_TM
    tok = lambda wd: pl.BlockSpec((1, tm, wd), lambda b, i: (b, i, 0))
    out_dtypes = (BF16, BF16, BF16, F32, BF16, BF16, BF16)
    return pl.pallas_call(
        functools.partial(_inproj_kernel, widths=widths),
        out_shape=[jax.ShapeDtypeStruct((bsz, LRU_BLOCKS, s, LRU_BLOCK) if idx == 3 else (bsz, s, wd), dt)
                   for idx, (wd, dt) in enumerate(zip(widths, out_dtypes))],
        grid=(bsz, s // tm),
        in_specs=[tok(d),
                  pl.BlockSpec((1, N_MOD, d), lambda b, i: (b, 0, 0)),
                  _resident((1, d)),
                  _resident(w_in.shape),
                  _resident((1, HEAD_DIM)),
                  _resident((1, HEAD_DIM))],
        out_specs=[pl.BlockSpec((1, LRU_BLOCKS, tm, LRU_BLOCK), lambda b, i: (b, 0, i, 0)) if idx == 3
                   else tok(wd) for idx, wd in enumerate(widths)],
        compiler_params=_params("parallel", "arbitrary"),
        name="inproj",
    )(x, mod, g1, w_in, qg, kg)


def _attn_kernel(q_ref, k_ref, v_ref, tri_ref, o_ref, acc_ref, carry_ref, tot_ref, z_ref):
    tq, tk = ATTN_TQ, ATTN_TK
    i = pl.program_id(2)
    heads = range(ATTN_HP)
    lanes = [slice(hh * HEAD_DIM, (hh + 1) * HEAD_DIM) for hh in heads]
    strips = [slice(r0, r0 + ATTN_RS) for r0 in range(0, tq, ATTN_RS)]
    acc_ref[...] = jnp.zeros_like(acc_ref)
    carry_ref[...] = jnp.zeros_like(carry_ref)

    def scores(qi, kj, which=heads):
        q0 = pl.multiple_of(qi * tq, tq)
        k0 = pl.multiple_of(kj * tk, tk)
        for hh in which:
            z_ref[hh] = lax.dot_general(
                q_ref[0, pl.ds(q0, tq), lanes[hh]], k_ref[0, pl.ds(k0, tk), lanes[hh]],
                (((1,), (1,)), ((), ())), preferred_element_type=F32)

    def scores_after(j, hh):
        last = j == 0
        nxt = jnp.minimum(i + 1, pl.num_programs(2) - 1)
        scores(jnp.where(last, nxt, i), jnp.where(last, nxt, j - 1), (hh,))

    def block(j, diagonal):
        k0 = pl.multiple_of(j * tk, tk)
        if diagonal:
            row = lax.broadcasted_iota(jnp.int32, (tq, tk), 0)
            col = lax.broadcasted_iota(jnp.int32, (tq, tk), 1)
            causal = col < row
        log_beta, suffix = {}, {}

        def logs(hh):
            lbs, loms = [], []
            for rs in strips:
                zs = z_ref[hh, rs, :].astype(BF16)
                sp = jnp.log(1.0 + jnp.exp2(-jnp.abs(zs))) * LOG2E
                lb = jnp.minimum(zs, 0.0) - sp
                log_om = -jnp.maximum(zs, 0.0) - sp
                if diagonal:
                    log_om = jnp.where(causal[rs], log_om, 0.0)
                tot = jnp.sum(log_om.astype(F32), axis=1, keepdims=True)
                tot_ref[hh, rs, :] = jnp.broadcast_to(tot, (ATTN_RS, LANES))
                lbs.append(lb)
                loms.append(log_om)
            suffix[hh] = _bdot(jnp.concatenate(loms, axis=0), tri_ref[...])
            log_beta[hh] = lbs

        def weights(hh):
            ws = []
            for si, rs in enumerate(strips):
                carry = carry_ref[hh, rs, :]
                e = (log_beta[hh][si].astype(F32) + suffix[hh][rs]
                     + jnp.concatenate([carry] * (tk // LANES), axis=1))
                w = jnp.exp2(e)
                if diagonal:
                    w = jnp.where(causal[rs], w, 0.0)
                ws.append(w.astype(BF16))
                carry_ref[hh, rs, :] = carry + tot_ref[hh, rs, :]
            acc_ref[hh] += _bdot(jnp.concatenate(ws, axis=0), v_ref[0, pl.ds(k0, tk), lanes[hh]])

        for step in range(ATTN_HP + ATTN_SKEW):
            if step < ATTN_HP:
                logs(step)
                scores_after(j, step)
            if step >= ATTN_SKEW:
                weights(step - ATTN_SKEW)

    @pl.when(i == 0)
    def _():
        scores(i, i)

    block(i, True)

    def body(it, _):
        block(i - 1 - it, False)
        return 0

    lax.fori_loop(0, i, body, 0)
    for hh in heads:
        o_ref[0, :, lanes[hh]] = acc_ref[hh].astype(o_ref.dtype)


def _attn(q, k, v):
    bsz, s, width = q.shape
    tq, tk = ATTN_TQ, ATTN_TK
    gw = ATTN_HP * HEAD_DIM
    assert tq == tk and s % tq == 0 and width % gw == 0
    r = lax.broadcasted_iota(jnp.int32, (tk, tk), 0)
    cidx = lax.broadcasted_iota(jnp.int32, (tk, tk), 1)
    tri = (r > cidx).astype(BF16)
    return pl.pallas_call(
        _attn_kernel,
        out_shape=jax.ShapeDtypeStruct((bsz, s, width), BF16),
        grid=(bsz, width // gw, s // tq),
        in_specs=[pl.BlockSpec((1, s, gw), lambda b, h, i: (b, 0, h)),
                  pl.BlockSpec((1, s, gw), lambda b, h, i: (b, 0, h)),
                  pl.BlockSpec((1, s, gw), lambda b, h, i: (b, 0, h)),
                  _resident((tk, tk))],
        out_specs=pl.BlockSpec((1, tq, gw), lambda b, h, i: (b, i, h)),
        scratch_shapes=[pltpu.VMEM((ATTN_HP, tq, HEAD_DIM), F32),
                        pltpu.VMEM((ATTN_HP, tq, LANES), F32),
                        pltpu.VMEM((ATTN_HP, tq, LANES), F32),
                        pltpu.VMEM((ATTN_HP, tq, tk), F32)],
        compiler_params=_params("parallel", "parallel", "arbitrary"),
        name="attn",
    )(q, k, v, tri)


def _sublane_scan(a, b, sub):
    for dist in (1, 2, 4):
        inside = sub >= dist
        a_prev = jnp.where(inside, pltpu.roll(a, dist, 0), 1.0)
        b_prev = jnp.where(inside, pltpu.roll(b, dist, 0), 0.0)
        b = a * b_prev + b
        a = a * a_prev
    return b


def _lru_kernel(xr_ref, gate_ref, cw_ref, cb_ref, wg_ref, bg_ref, lam_ref, y_ref,
                tail_ref, hlast_ref, h_s):
    ts, seg = LRU_TS, LRU_SEG
    hist = CONV_WIDTH - 1

    @pl.when(pl.program_id(1) == 0)
    def _():
        tail_ref[...] = jnp.zeros_like(tail_ref)
        hlast_ref[...] = jnp.zeros_like(hlast_ref)

    sub = lax.broadcasted_iota(jnp.int32, (SUBLANES, LRU_BLOCK), 0)
    sub_hist = lax.broadcasted_iota(jnp.int32, (hist, SUBLANES, LRU_BLOCK), 1)

    for n in range(LRU_BLOCKS):
        sl = slice(n * LRU_BLOCK, (n + 1) * LRU_BLOCK)
        x = xr_ref[0, n]
        rolled = pltpu.roll(x[ts - hist * SUBLANES:].reshape(hist, SUBLANES, LRU_BLOCK), 1, 1)
        before = jnp.where(sub_hist == 0, tail_ref[n], rolled)
        tail_ref[n] = rolled
        xe = jnp.concatenate([before.reshape(hist * SUBLANES, LRU_BLOCK), x], axis=0)
        xc = cb_ref[:, sl]
        for kk in range(CONV_WIDTH):
            xc = xc + xe[kk * SUBLANES:kk * SUBLANES + ts] * cw_ref[kk:kk + 1, sl]
        th = jnp.tanh(_bdot(xc.astype(BF16), wg_ref[n]) + bg_ref[n])
        ig = 0.5 * th[:, LRU_BLOCK:] + 0.5
        neg_lam = -lam_ref[:, sl]
        softplus = jnp.maximum(neg_lam, 0.0) + jnp.log(1.0 + jnp.exp(-jnp.abs(neg_lam)))
        half_c = (0.5 * LRU_C) * softplus
        nla = half_c * th[:, :LRU_BLOCK] + half_c
        a = jnp.exp2(nla * -LOG2E)
        m = jnp.tanh(nla) * (1.0 + a * a)
        mult = m * lax.rsqrt(jnp.maximum(m, F32_MIN_NORMAL))
        u = mult * (ig * xc)
        local, decay = [u[:SUBLANES]], [a[:SUBLANES]]
        for r in range(1, seg):
            rows = slice(r * SUBLANES, (r + 1) * SUBLANES)
            local.append(a[rows] * local[-1] + u[rows])
            decay.append(a[rows] * decay[-1])
        chunk_entry = jnp.broadcast_to(hlast_ref[n, SUBLANES - 1:SUBLANES, :], (SUBLANES, LRU_BLOCK))
        first = sub == 0
        entry = _sublane_scan(jnp.where(first, 0.0, pltpu.roll(decay[-1], 1, 0)),
                              jnp.where(first, chunk_entry, pltpu.roll(local[-1], 1, 0)), sub)
        hlast_ref[n] = local[-1] + decay[-1] * entry
        for r in range(seg):
            h_s[n, r * SUBLANES:(r + 1) * SUBLANES, :] = local[r] + decay[r] * entry
        h = jnp.concatenate(
            [h_s[n, pl.ds(r0 * SUBLANES + sg, SUBLANES, stride=SUBLANES), :]
             for sg in range(SUBLANES) for r0 in range(0, seg, SUBLANES)], axis=0)
        y_ref[0, :, sl] = (gate_ref[0, :, sl].astype(F32) * h).astype(y_ref.dtype)


def _lru(xr, gate, conv_w, conv_b, w_gates, b_gates, lam):
    bsz, nblk, s, blk = xr.shape
    ts = LRU_TS
    tok = pl.BlockSpec((1, ts, nblk * blk), lambda b, i: (b, i, 0))
    return pl.pallas_call(
        _lru_kernel,
        out_shape=jax.ShapeDtypeStruct((bsz, s, nblk * blk), BF16),
        grid=(bsz, s // ts),
        in_specs=[pl.BlockSpec((1, nblk, ts, blk), lambda b, i: (b, 0, i, 0)), tok,
                  _resident(conv_w.shape), _resident(conv_b.shape),
                  _resident(w_gates.shape), _resident(b_gates.shape), _resident(lam.shape)],
        out_specs=tok,
        scratch_shapes=[pltpu.VMEM((nblk, CONV_WIDTH - 1, SUBLANES, blk), F32),
                        pltpu.VMEM((nblk, SUBLANES, blk), F32),
                        pltpu.VMEM((nblk, ts, blk), F32)],
        compiler_params=_params("parallel", "arbitrary"),
        name="lru",
    )(xr, gate, conv_w, conv_b, w_gates, b_gates, lam)


def _merge_kernel(o_ref, y_ref, ga_ref, gb_ref, x_ref, mod_ref, wpa_ref, wpl_ref, wo_ref, x1_ref):
    p_attn = _bdot(o_ref[0], wpa_ref[...])
    p_lru = _bdot(y_ref[0], wpl_ref[...])
    merged = ga_ref[0].astype(F32) * p_attn + gb_ref[0].astype(F32) * p_lru
    mix = _bdot(merged.astype(BF16), wo_ref[...])
    gate1 = mod_ref[0, 2:3, :]
    x1_ref[0] = x_ref[0] + gate1 * mix


def _merge(o, y, ga, gb, x, mod, wpa, wpl, wo):
    bsz, s, d = x.shape
    tm = MERGE_TM
    tok = lambda wd: pl.BlockSpec((1, tm, wd), lambda b, i: (b, i, 0))
    return pl.pallas_call(
        _merge_kernel,
        out_shape=jax.ShapeDtypeStruct((bsz, s, d), F32),
        grid=(bsz, s // tm),
        in_specs=[tok(o.shape[2]), tok(y.shape[2]), tok(d), tok(d), tok(d),
                  pl.BlockSpec((1, N_MOD, d), lambda b, i: (b, 0, 0)),
                  _resident(wpa.shape), _resident(wpl.shape), _resident(wo.shape)],
        out_specs=tok(d),
        compiler_params=_params("parallel", "arbitrary"),
        name="merge",
    )(o, y, ga, gb, x, mod, wpa, wpl, wo)


def _ffn_kernel(x_ref, mod_ref, g2_ref, wg_ref, wu_ref, wd_ref, out_ref):
    x = x_ref[0]
    shift = mod_ref[0, 3:4, :]
    scale = mod_ref[0, 4:5, :]
    gate2 = mod_ref[0, 5:6, :]
    h = (_rms(x, g2_ref[...]) * (1.0 + scale) + shift).astype(BF16)
    hidden = wg_ref.shape[1]
    acc = jnp.zeros(x.shape, F32)
    for c0 in range(0, hidden, FFN_CHUNK):
        g = _bdot(h, wg_ref[:, c0:c0 + FFN_CHUNK])
        u = _bdot(h, wu_ref[:, c0:c0 + FFN_CHUNK])
        half = 0.5 * g
        act = ((half + half * jnp.tanh(half)) * u).astype(BF16)
        acc = acc + _bdot(act, wd_ref[c0:c0 + FFN_CHUNK, :])
    out_ref[0] = x + gate2 * acc


def _ffn(x1, mod, g2, wg, wu, wd):
    bsz, s, d = x1.shape
    tm = FFN_TM
    assert wg.shape[1] % FFN_CHUNK == 0
    tok = pl.BlockSpec((1, tm, d), lambda b, i: (b, i, 0))
    return pl.pallas_call(
        _ffn_kernel,
        out_shape=jax.ShapeDtypeStruct((bsz, s, d), F32),
        grid=(bsz, s // tm),
        in_specs=[tok,
                  pl.BlockSpec((1, N_MOD, d), lambda b, i: (b, 0, 0)),
                  _resident((1, d)),
                  _resident(wg.shape), _resident(wu.shape), _resident(wd.shape)],
        out_specs=tok,
        compiler_params=_params("parallel", "arbitrary"),
        name="ffn",
    )(x1, mod, g2, wg, wu, wd)


def kernel(x, c, w_ada, b_ada, norm1_g, w_in, q_norm_g, k_norm_g, conv_w, conv_b,
           w_rg, b_rg, w_ig, b_ig, lru_lambda, w_proj_attn, w_proj_lru, w_out,
           norm2_g, w_ffn_in, w_ffn_out):
    bsz, s, d = x.shape
    depth = w_ada.shape[0]
    attn_w = N_HEADS * HEAD_DIM
    lru_w = LRU_BLOCKS * LRU_BLOCK
    widths = (attn_w, attn_w, attn_w, lru_w, lru_w, d, d)
    hidden = w_ffn_out.shape[1]
    for l in range(depth):
        mod = _ada(c, w_ada[l], b_ada[l]).reshape(bsz, N_MOD, d)
        q, k, v, xr, gate, ga, gb = _inproj(
            x, mod, norm1_g[l].reshape(1, d), w_in[l].astype(BF16),
            q_norm_g[l].reshape(1, HEAD_DIM), k_norm_g[l].reshape(1, HEAD_DIM), widths)
        o = _attn(q, k, v)
        w_gates = (0.5 * jnp.concatenate([w_rg[l], w_ig[l]], axis=-1)).astype(BF16)
        b_gates = 0.5 * jnp.concatenate([b_rg[l].reshape(LRU_BLOCKS, 1, LRU_BLOCK),
                                         b_ig[l].reshape(LRU_BLOCKS, 1, LRU_BLOCK)], axis=-1)
        y = _lru(xr, gate, conv_w[l], conv_b[l].reshape(1, lru_w), w_gates, b_gates,
                 lru_lambda[l].reshape(1, lru_w))
        x1 = _merge(o, y, ga, gb, x, mod, w_proj_attn[l].astype(BF16),
                    w_proj_lru[l].astype(BF16), w_out[l].astype(BF16))
        w_ffn = w_ffn_in[l].astype(BF16)
        x = _ffn(x1, mod, norm2_g[l].reshape(1, d), w_ffn[:, :hidden], w_ffn[:, hidden:],
                 w_ffn_out[l].astype(BF16))
    return x
```

```python
import functools

import jax
import jax.numpy as jnp
from jax import lax
from jax.experimental import pallas as pl
from jax.experimental.pallas import tpu as pltpu

F32 = jnp.float32
BF16 = jnp.bfloat16

N_HEADS = 8
HEAD_DIM = 128
LRU_BLOCKS = 12
LRU_BLOCK = 128
CONV_WIDTH = 4
LRU_C = 8.0
N_MOD = 6
EPS = 1e-6
LOG2E = 1.4426950408889634
LOG2E_HI = 1.4453125
LOG2E_LO = LOG2E - LOG2E_HI
F32_MIN_NORMAL = 2.0 ** -126

V7X_VMEM_LIMIT_BYTES = 56 * 1024 * 1024
LANES = 128
SUBLANES = 8

ADA_TN = 1024
INPROJ_TM = 512
ATTN_TQ = 256
ATTN_TK = 256
ATTN_HP = 4
ATTN_RS = 32
ATTN_SKEW = 3
LRU_TS = 256
LRU_SEG = LRU_TS // SUBLANES
MERGE_TM = 512
FFN_TM = 512
FFN_CHUNK = 1408


def _params(*sem):
    return pltpu.CompilerParams(dimension_semantics=sem, vmem_limit_bytes=V7X_VMEM_LIMIT_BYTES)


def _resident(shape):
    nd = len(shape)
    return pl.BlockSpec(shape, lambda *_: (0,) * nd, pipeline_mode=pl.Buffered(1))


def _bdot(a, b):
    return jnp.dot(a, b, preferred_element_type=F32)


def _rms(x, g):
    ms = jnp.mean(x * x, axis=-1, keepdims=True)
    return x * lax.rsqrt(ms + EPS) * g


def _ada_kernel(c_ref, w_ref, b_ref, o_ref):
    c = c_ref[...]
    ca = c * jax.nn.sigmoid(c)
    o_ref[...] = jnp.dot(ca, w_ref[...], preferred_element_type=F32,
                         precision=lax.Precision.HIGHEST) + b_ref[...]


def _ada(c, w, b):
    bsz, d = c.shape
    n = w.shape[1]
    return pl.pallas_call(
        _ada_kernel,
        out_shape=jax.ShapeDtypeStruct((bsz, n), F32),
        grid=(n // ADA_TN,),
        in_specs=[pl.BlockSpec((bsz, d), lambda j: (0, 0)),
                  pl.BlockSpec((d, ADA_TN), lambda j: (0, j)),
                  pl.BlockSpec((1, ADA_TN), lambda j: (0, j))],
        out_specs=pl.BlockSpec((bsz, ADA_TN), lambda j: (0, j)),
        compiler_params=_params("arbitrary"),
        name="ada",
    )(c, w, b.reshape(1, n))


def _inproj_kernel(x_ref, mod_ref, g1_ref, w_ref, qg_ref, kg_ref,
                   q_ref, k_ref, v_ref, xr_ref, gr_ref, ga_ref, gb_ref, *, widths):
    x = x_ref[0]
    shift = mod_ref[0, 0:1, :]
    scale = mod_ref[0, 1:2, :]
    h = (_rms(x, g1_ref[...]) * (1.0 + scale) + shift).astype(BF16)

    offs = [0]
    for wd in widths:
        offs.append(offs[-1] + wd)

    def proj(idx):
        return _bdot(h, w_ref[:, offs[idx]:offs[idx + 1]])

    gr_ref[0] = jax.nn.gelu(proj(4)).astype(BF16)
    ga_ref[0] = (0.5 * jnp.tanh(0.5 * proj(5)) + 0.5).astype(BF16)
    gb_ref[0] = (0.5 * jnp.tanh(0.5 * proj(6)) + 0.5).astype(BF16)
    q_scale = HEAD_DIM ** -0.5 * LOG2E
    for idx, g_ref, o_ref, mul in ((0, qg_ref, q_ref, q_scale), (1, kg_ref, k_ref, 1.0)):
        p = proj(idx)
        for hh in range(N_HEADS):
            sl = slice(hh * HEAD_DIM, (hh + 1) * HEAD_DIM)
            ph = _rms(p[:, sl], g_ref[...])
            if mul != 1.0:
                ph = ph * mul
            o_ref[0, :, sl] = ph.astype(BF16)
    v_ref[0] = proj(2).astype(BF16)
    xr = proj(3)
    for n in range(LRU_BLOCKS):
        for c0 in range(0, xr.shape[0], LRU_TS):
            for sg in range(SUBLANES):
                t0 = c0 + sg * LRU_SEG
                xr_ref[0, n, pl.ds(c0 + sg, LRU_SEG, stride=SUBLANES), :] = (
                    xr[t0:t0 + LRU_SEG, n * LRU_BLOCK:(n + 1) * LRU_BLOCK])


def _inproj(x, mod, g1, w_in, qg, kg, widths):
    bsz, s, d = x.shape
    tm = INPROJ_TM
    tok = lambda wd: pl.BlockSpec((1, tm, wd), lambda b, i: (b, i, 0))
    out_dtypes = (BF16, BF16, BF16, F32, BF16, BF16, BF16)
    return pl.pallas_call(
        functools.partial(_inproj_kernel, widths=widths),
        out_shape=[jax.ShapeDtypeStruct((bsz, LRU_BLOCKS, s, LRU_BLOCK) if idx == 3 else (bsz, s, wd), dt)
                   for idx, (wd, dt) in enumerate(zip(widths, out_dtypes))],
        grid=(bsz, s // tm),
        in_specs=[tok(d),
                  pl.BlockSpec((1, N_MOD, d), lambda b, i: (b, 0, 0)),
                  _resident((1, d)),
                  _resident(w_in.shape),
                  _resident((1, HEAD_DIM)),
                  _resident((1, HEAD_DIM))],
        out_specs=[pl.BlockSpec((1, LRU_BLOCKS, tm, LRU_BLOCK), lambda b, i: (b, 0, i, 0)) if idx == 3
                   else tok(wd) for idx, wd in enumerate(widths)],
        compiler_params=_params("parallel", "arbitrary"),
        name="inproj",
    )(x, mod, g1, w_in, qg, kg)


def _attn_kernel(q_ref, k_ref, v_ref, tri_ref, o_ref, acc_ref, carry_ref, tot_ref, z_ref):
    tq, tk = ATTN_TQ, ATTN_TK
    i = pl.program_id(2)
    heads = range(ATTN_HP)
    lanes = [slice(hh * HEAD_DIM, (hh + 1) * HEAD_DIM) for hh in heads]
    strips = [slice(r0, r0 + ATTN_RS) for r0 in range(0, tq, ATTN_RS)]
    acc_ref[...] = jnp.zeros_like(acc_ref)
    carry_ref[...] = jnp.zeros_like(carry_ref)

    def scores(qi, kj, which=heads):
        q0 = pl.multiple_of(qi * tq, tq)
        k0 = pl.multiple_of(kj * tk, tk)
        for hh in which:
            z_ref[hh] = lax.dot_general(
                q_ref[0, pl.ds(q0, tq), lanes[hh]], k_ref[0, pl.ds(k0, tk), lanes[hh]],
                (((1,), (1,)), ((), ())), preferred_element_type=F32)

    def scores_after(j, hh):
        last = j == 0
        nxt = jnp.minimum(i + 1, pl.num_programs(2) - 1)
        scores(jnp.where(last, nxt, i), jnp.where(last, nxt, j - 1), (hh,))

    def block(j, diagonal):
        k0 = pl.multiple_of(j * tk, tk)
        if diagonal:
            row = lax.broadcasted_iota(jnp.int32, (tq, tk), 0)
            col = lax.broadcasted_iota(jnp.int32, (tq, tk), 1)
            causal = col < row
        log_beta, suffix = {}, {}

        def logs(hh):
            lbs, loms = [], []
            for rs in strips:
                zs = z_ref[hh, rs, :].astype(BF16)
                ln = jnp.log(1.0 + jnp.exp2(-jnp.abs(zs)))
                sp = ln * LOG2E_HI + ln * LOG2E_LO
                lb = jnp.minimum(zs, 0.0) - sp
                log_om = -jnp.maximum(zs, 0.0) - sp
                if diagonal:
                    log_om = jnp.where(causal[rs], log_om, 0.0)
                tot = jnp.sum(log_om.astype(F32), axis=1, keepdims=True)
                tot_ref[hh, rs, :] = jnp.broadcast_to(tot, (ATTN_RS, LANES))
                lbs.append(lb)
                loms.append(log_om)
            suffix[hh] = _bdot(jnp.concatenate(loms, axis=0), tri_ref[...])
            log_beta[hh] = lbs

        def weights(hh):
            ws = []
            for si, rs in enumerate(strips):
                carry = carry_ref[hh, rs, :]
                e = (log_beta[hh][si].astype(F32) + suffix[hh][rs]
                     + jnp.concatenate([carry] * (tk // LANES), axis=1))
                w = jnp.exp2(e)
                if diagonal:
                    w = jnp.where(causal[rs], w, 0.0)
                ws.append(w.astype(BF16))
                carry_ref[hh, rs, :] = carry + tot_ref[hh, rs, :]
            acc_ref[hh] += _bdot(jnp.concatenate(ws, axis=0), v_ref[0, pl.ds(k0, tk), lanes[hh]])

        for step in range(ATTN_HP + ATTN_SKEW):
            if step < ATTN_HP:
                logs(step)
                scores_after(j, step)
            if step >= ATTN_SKEW:
                weights(step - ATTN_SKEW)

    @pl.when(i == 0)
    def _():
        scores(i, i)

    block(i, True)

    def body(it, _):
        block(i - 1 - it, False)
        return 0

    lax.fori_loop(0, i, body, 0)
    for hh in heads:
        o_ref[0, :, lanes[hh]] = acc_ref[hh].astype(o_ref.dtype)


def _attn(q, k, v):
    bsz, s, width = q.shape
    tq, tk = ATTN_TQ, ATTN_TK
    gw = ATTN_HP * HEAD_DIM
    assert tq == tk and s % tq == 0 and width % gw == 0
    r = lax.broadcasted_iota(jnp.int32, (tk, tk), 0)
    cidx = lax.broadcasted_iota(jnp.int32, (tk, tk), 1)
    tri = (r > cidx).astype(BF16)
    return pl.pallas_call(
        _attn_kernel,
        out_shape=jax.ShapeDtypeStruct((bsz, s, width), BF16),
        grid=(bsz, width // gw, s // tq),
        in_specs=[pl.BlockSpec((1, s, gw), lambda b, h, i: (b, 0, h)),
                  pl.BlockSpec((1, s, gw), lambda b, h, i: (b, 0, h)),
                  pl.BlockSpec((1, s, gw), lambda b, h, i: (b, 0, h)),
                  _resident((tk, tk))],
        out_specs=pl.BlockSpec((1, tq, gw), lambda b, h, i: (b, i, h)),
        scratch_shapes=[pltpu.VMEM((ATTN_HP, tq, HEAD_DIM), F32),
                        pltpu.VMEM((ATTN_HP, tq, LANES), F32),
                        pltpu.VMEM((ATTN_HP, tq, LANES), F32),
                        pltpu.VMEM((ATTN_HP, tq, tk), F32)],
        compiler_params=_params("parallel", "parallel", "arbitrary"),
        name="attn",
    )(q, k, v, tri)


def _sublane_scan(a, b, sub):
    for dist in (1, 2, 4):
        inside = sub >= dist
        a_prev = jnp.where(inside, pltpu.roll(a, dist, 0), 1.0)
        b_prev = jnp.where(inside, pltpu.roll(b, dist, 0), 0.0)
        b = a * b_prev + b
        a = a * a_prev
    return b


def _lru_kernel(xr_ref, gate_ref, cw_ref, cb_ref, wg_ref, bg_ref, lam_ref, y_ref,
                tail_ref, hlast_ref, h_s):
    ts, seg = LRU_TS, LRU_SEG
    hist = CONV_WIDTH - 1

    @pl.when(pl.program_id(1) == 0)
    def _():
        tail_ref[...] = jnp.zeros_like(tail_ref)
        hlast_ref[...] = jnp.zeros_like(hlast_ref)

    sub = lax.broadcasted_iota(jnp.int32, (SUBLANES, LRU_BLOCK), 0)
    sub_hist = lax.broadcasted_iota(jnp.int32, (hist, SUBLANES, LRU_BLOCK), 1)

    for n in range(LRU_BLOCKS):
        sl = slice(n * LRU_BLOCK, (n + 1) * LRU_BLOCK)
        x = xr_ref[0, n]
        rolled = pltpu.roll(x[ts - hist * SUBLANES:].reshape(hist, SUBLANES, LRU_BLOCK), 1, 1)
        before = jnp.where(sub_hist == 0, tail_ref[n], rolled)
        tail_ref[n] = rolled
        xe = jnp.concatenate([before.reshape(hist * SUBLANES, LRU_BLOCK), x], axis=0)
        xc = cb_ref[:, sl]
        for kk in range(CONV_WIDTH):
            xc = xc + xe[kk * SUBLANES:kk * SUBLANES + ts] * cw_ref[kk:kk + 1, sl]
        th = jnp.tanh(_bdot(xc.astype(BF16), wg_ref[n]) + bg_ref[n])
        ig = 0.5 * th[:, LRU_BLOCK:] + 0.5
        neg_lam = -lam_ref[:, sl]
        softplus = jnp.maximum(neg_lam, 0.0) + jnp.log(1.0 + jnp.exp(-jnp.abs(neg_lam)))
        half_c = (0.5 * LRU_C) * softplus
        nla = half_c * th[:, :LRU_BLOCK] + half_c
        a = jnp.exp2(nla * -LOG2E)
        m = jnp.tanh(nla) * (1.0 + a * a)
        mult = m * lax.rsqrt(jnp.maximum(m, F32_MIN_NORMAL))
        u = mult * (ig * xc)
        local, decay = [u[:SUBLANES]], [a[:SUBLANES]]
        for r in range(1, seg):
            rows = slice(r * SUBLANES, (r + 1) * SUBLANES)
            local.append(a[rows] * local[-1] + u[rows])
            decay.append(a[rows] * decay[-1])
        chunk_entry = jnp.broadcast_to(hlast_ref[n, SUBLANES - 1:SUBLANES, :], (SUBLANES, LRU_BLOCK))
        first = sub == 0
        entry = _sublane_scan(jnp.where(first, 0.0, pltpu.roll(decay[-1], 1, 0)),
                              jnp.where(first, chunk_entry, pltpu.roll(local[-1], 1, 0)), sub)
        hlast_ref[n] = local[-1] + decay[-1] * entry
        for r in range(seg):
            h_s[n, r * SUBLANES:(r + 1) * SUBLANES, :] = local[r] + decay[r] * entry
        h = jnp.concatenate(
            [h_s[n, pl.ds(r0 * SUBLANES + sg, SUBLANES, stride=SUBLANES), :]
             for sg in range(SUBLANES) for r0 in range(0, seg, SUBLANES)], axis=0)
        y_ref[0, :, sl] = (gate_ref[0, :, sl].astype(F32) * h).astype(y_ref.dtype)


def _lru(xr, gate, conv_w, conv_b, w_gates, b_gates, lam):
    bsz, nblk, s, blk = xr.shape
    ts = LRU_TS
    tok = pl.BlockSpec((1, ts, nblk * blk), lambda b, i: (b, i, 0))
    return pl.pallas_call(
        _lru_kernel,
        out_shape=jax.ShapeDtypeStruct((bsz, s, nblk * blk), BF16),
        grid=(bsz, s // ts),
        in_specs=[pl.BlockSpec((1, nblk, ts, blk), lambda b, i: (b, 0, i, 0)), tok,
                  _resident(conv_w.shape), _resident(conv_b.shape),
                  _resident(w_gates.shape), _resident(b_gates.shape), _resident(lam.shape)],
        out_specs=tok,
        scratch_shapes=[pltpu.VMEM((nblk, CONV_WIDTH - 1, SUBLANES, blk), F32),
                        pltpu.VMEM((nblk, SUBLANES, blk), F32),
                        pltpu.VMEM((nblk, ts, blk), F32)],
        compiler_params=_params("parallel", "arbitrary"),
        name="lru",
    )(xr, gate, conv_w, conv_b, w_gates, b_gates, lam)


def _merge_kernel(o_ref, y_ref, ga_ref, gb_ref, x_ref, mod_ref, wpa_ref, wpl_ref, wo_ref, x1_ref):
    p_attn = _bdot(o_ref[0], wpa_ref[...])
    p_lru = _bdot(y_ref[0], wpl_ref[...])
    merged = ga_ref[0].astype(F32) * p_attn + gb_ref[0].astype(F32) * p_lru
    mix = _bdot(merged.astype(BF16), wo_ref[...])
    gate1 = mod_ref[0, 2:3, :]
    x1_ref[0] = x_ref[0] + gate1 * mix


def _merge(o, y, ga, gb, x, mod, wpa, wpl, wo):
    bsz, s, d = x.shape
    tm = MERGE_TM
    tok = lambda wd: pl.BlockSpec((1, tm, wd), lambda b, i: (b, i, 0))
    return pl.pallas_call(
        _merge_kernel,
        out_shape=jax.ShapeDtypeStruct((bsz, s, d), F32),
        grid=(bsz, s // tm),
        in_specs=[tok(o.shape[2]), tok(y.shape[2]), tok(d), tok(d), tok(d),
                  pl.BlockSpec((1, N_MOD, d), lambda b, i: (b, 0, 0)),
                  _resident(wpa.shape), _resident(wpl.shape), _resident(wo.shape)],
        out_specs=tok(d),
        compiler_params=_params("parallel", "arbitrary"),
        name="merge",
    )(o, y, ga, gb, x, mod, wpa, wpl, wo)


def _ffn_kernel(x_ref, mod_ref, g2_ref, wg_ref, wu_ref, wd_ref, out_ref):
    x = x_ref[0]
    shift = mod_ref[0, 3:4, :]
    scale = mod_ref[0, 4:5, :]
    gate2 = mod_ref[0, 5:6, :]
    h = (_rms(x, g2_ref[...]) * (1.0 + scale) + shift).astype(BF16)
    hidden = wg_ref.shape[1]
    acc = jnp.zeros(x.shape, F32)
    for c0 in range(0, hidden, FFN_CHUNK):
        g = _bdot(h, wg_ref[:, c0:c0 + FFN_CHUNK])
        u = _bdot(h, wu_ref[:, c0:c0 + FFN_CHUNK])
        half = 0.5 * g
        act = ((half + half * jnp.tanh(half)) * u).astype(BF16)
        acc = acc + _bdot(act, wd_ref[c0:c0 + FFN_CHUNK, :])
    out_ref[0] = x + gate2 * acc


def _ffn(x1, mod, g2, wg, wu, wd):
    bsz, s, d = x1.shape
    tm = FFN_TM
    assert wg.shape[1] % FFN_CHUNK == 0
    tok = pl.BlockSpec((1, tm, d), lambda b, i: (b, i, 0))
    return pl.pallas_call(
        _ffn_kernel,
        out_shape=jax.ShapeDtypeStruct((bsz, s, d), F32),
        grid=(bsz, s // tm),
        in_specs=[tok,
                  pl.BlockSpec((1, N_MOD, d), lambda b, i: (b, 0, 0)),
                  _resident((1, d)),
                  _resident(wg.shape), _resident(wu.shape), _resident(wd.shape)],
        out_specs=tok,
        compiler_params=_params("parallel", "arbitrary"),
        name="ffn",
    )(x1, mod, g2, wg, wu, wd)


def kernel(x, c, w_ada, b_ada, norm1_g, w_in, q_norm_g, k_norm_g, conv_w, conv_b,
           w_rg, b_rg, w_ig, b_ig, lru_lambda, w_proj_attn, w_proj_lru, w_out,
           norm2_g, w_ffn_in, w_ffn_out):
    bsz, s, d = x.shape
    depth = w_ada.shape[0]
    attn_w = N_HEADS * HEAD_DIM
    lru_w = LRU_BLOCKS * LRU_BLOCK
    widths = (attn_w, attn_w, attn_w, lru_w, lru_w, d, d)
    hidden = w_ffn_out.shape[1]
    for l in range(depth):
        mod = _ada(c, w_ada[l], b_ada[l]).reshape(bsz, N_MOD, d)
        q, k, v, xr, gate, ga, gb = _inproj(
            x, mod, norm1_g[l].reshape(1, d), w_in[l].astype(BF16),
            q_norm_g[l].reshape(1, HEAD_DIM), k_norm_g[l].reshape(1, HEAD_DIM), widths)
        o = _attn(q, k, v)
        w_gates = (0.5 * jnp.concatenate([w_rg[l], w_ig[l]], axis=-1)).astype(BF16)
        b_gates = 0.5 * jnp.concatenate([b_rg[l].reshape(LRU_BLOCKS, 1, LRU_BLOCK),
                                         b_ig[l].reshape(LRU_BLOCKS, 1, LRU_BLOCK)], axis=-1)
        y = _lru(xr, gate, conv_w[l], conv_b[l].reshape(1, lru_w), w_gates, b_gates,
                 lru_lambda[l].reshape(1, lru_w))
        x1 = _merge(o, y, ga, gb, x, mod, w_proj_attn[l].astype(BF16),
                    w_proj_lru[l].astype(BF16), w_out[l].astype(BF16))
        w_ffn = w_ffn_in[l].astype(BF16)
        x = _ffn(x1, mod, norm2_g[l].reshape(1, d), w_ffn[:, :hidden], w_ffn[:, hidden:],
                 w_ffn_out[l].astype(BF16))
    return x
```
